```python
import math
import jax, jax.numpy as jnp
from jax import lax
import numpy as np

D_MODEL = 1024
BATCH = 8
SEQ = 4096
DEPTH = 2
DEC_BATCH = 32
DEC_SEQ = 1
PAST_LEN = 16384
PAGE_SIZE = 128

N_A_LAYERS = DEPTH // 2
N_B_LAYERS = DEPTH - N_A_LAYERS
POOL_WINDOWS = (2, 4, 8, 16)
N_POOL_GROUPS = len(POOL_WINDOWS)
POOL_GROUP = D_MODEL // N_POOL_GROUPS
POOL_BUF = max(POOL_WINDOWS) - 1
N_HEADS = 16
N_KV = 4
HPG = N_HEADS // N_KV
HEAD_DIM = D_MODEL // N_HEADS
ROT_DIM = HEAD_DIM // 4
ROPE_THETA = 500000.0
L_CMP = 32
STRIDE = 16
CMP_PARTS = L_CMP // STRIDE
CMP_HIDDEN = 2 * HEAD_DIM
L_SEL = 64
SEL_RATIO = L_SEL // STRIDE
N_SEL = 16
WINDOW = 512
N_BRANCH = 3
D_FF = 4 * D_MODEL
NSA_QBLOCK = 32
RMS_EPS = 1e-6
NEG_INF = -1e30
FORCE_SCORE = 1e9

kernel_name = 'yoco_pool_nsa_step'


def rmsnorm(x, g):
    xf = x.astype(jnp.float32)
    y = xf * lax.rsqrt(jnp.mean(xf * xf, axis=-1, keepdims=True) + RMS_EPS)
    return (y * g.astype(jnp.float32)).astype(x.dtype)


def rope(x, pos):
    half = ROT_DIM // 2
    inv = ROPE_THETA ** (-jnp.arange(0, ROT_DIM, 2, dtype=jnp.float32) / ROT_DIM)
    ang = pos.astype(jnp.float32)[:, None] * inv[None, :]
    shp = (pos.shape[0],) + (1,) * (x.ndim - 3) + (half,)
    cos = jnp.cos(ang).reshape(shp)
    sin = jnp.sin(ang).reshape(shp)
    xr = x[..., :ROT_DIM].astype(jnp.float32)
    x1, x2 = xr[..., :half], xr[..., half:]
    rot = jnp.concatenate([x1 * cos - x2 * sin, x2 * cos + x1 * sin], axis=-1)
    return jnp.concatenate([rot.astype(x.dtype), x[..., ROT_DIM:]], axis=-1)


def pool_mix(u_ext, n_hist, w_pool, scale):
    n_tot = u_ext.shape[1]
    cs0 = jnp.pad(jnp.cumsum(u_ext.astype(jnp.float32), axis=1), ((0, 0), (1, 0), (0, 0)))
    idx = jnp.arange(n_hist, n_tot)
    cur = u_ext[:, n_hist:].astype(jnp.float32)
    outs = []
    for gi, w in enumerate(POOL_WINDOWS):
        sl = slice(gi * POOL_GROUP, (gi + 1) * POOL_GROUP)
        c = cs0[..., sl]
        upper = c[:, n_hist + 1:]
        lower = jnp.pad(c, ((0, 0), (w, 0), (0, 0)))[:, n_hist + 1:n_tot + 1]
        cnt = jnp.minimum(idx + 1, w).astype(jnp.float32)[None, :, None]
        outs.append((upper - lower) / cnt - cur[..., sl])
    d = jnp.stack(outs, axis=2).astype(u_ext.dtype)
    z = jnp.einsum('btgc,gcd->btgd', d, w_pool).reshape(d.shape[0], d.shape[1], D_MODEL)
    return z * scale


def sq_relu_mlp(h, g, w_up, w_down):
    a = jax.nn.relu(rmsnorm(h, g) @ w_up)
    return (a * a) @ w_down


def kv_side(h, pos, g_kv, w_kv):
    B, T, _ = h.shape
    kv = (rmsnorm(h, g_kv) @ w_kv).reshape(B, T, N_BRANCH, 2, N_KV, HEAD_DIM)
    k = rope(kv[:, :, :, 0], pos)
    kv = jnp.stack([k, kv[:, :, :, 1]], axis=3)
    return kv[:, :, 0], kv[:, :, 1], kv[:, :, 2]


def cmp_parts(rows, cmp_w1):
    B, T = rows.shape[:2]
    n_seg = T // STRIDE
    seg = rows[:, :n_seg * STRIDE].reshape(B, n_seg, STRIDE, 2, N_KV, HEAD_DIM)
    w1 = cmp_w1.reshape(2, CMP_PARTS, STRIDE, HEAD_DIM, CMP_HIDDEN)
    return jnp.einsum('bsregh,eprhk->bspegk', seg, w1)


def cmp_finish(parts, cmp_pe, cmp_w1, cmp_w2):
    n_seg = parts.shape[1]
    nc = n_seg - CMP_PARTS + 1
    w1 = cmp_w1.reshape(2, CMP_PARTS, STRIDE, HEAD_DIM, CMP_HIDDEN)
    pe = cmp_pe.reshape(CMP_PARTS, STRIDE, 2, HEAD_DIM)
    pe_part = jnp.einsum('preh,eprhk->pek', pe, w1)
    hid = parts[:, 0:nc, 0] + pe_part[0][None, None, :, None, :]
    for p in range(1, CMP_PARTS):
        hid = hid + parts[:, p:p + nc, p] + pe_part[p][None, None, :, None, :]
    hid = jax.nn.gelu(hid)
    return jnp.einsum('bnegk,ekh->bnegh', hid, cmp_w2)


def cmp_end_positions(ckv):
    return jnp.arange(ckv.shape[1]) * STRIDE + (L_CMP - 1)


def cmp_to_sel(imp, ns):
    nc = imp.shape[-1]
    padded = jnp.pad(imp, ((0, 0),) * (imp.ndim - 1) + ((CMP_PARTS - 1, SEL_RATIO * ns - nc),))
    out = jnp.zeros(imp.shape[:-1] + (ns,), imp.dtype)
    for m in range(SEL_RATIO):
        for n in range(CMP_PARTS):
            off = m - n + CMP_PARTS - 1
            out = out + padded[..., off:off + SEL_RATIO * (ns - 1) + 1:SEL_RATIO]
    return out


def to_sel_blocks(rows, ns):
    B, T = rows.shape[:2]
    x = jnp.pad(rows, ((0, 0), (0, ns * L_SEL - T), (0, 0), (0, 0), (0, 0)))
    x = x.reshape(B, ns, L_SEL, 2, N_KV, HEAD_DIM).transpose(0, 4, 1, 2, 3, 5)
    return x.reshape(B, N_KV, ns, L_SEL * 2 * HEAD_DIM)


def block_gatherer(sblk):
    B, G = sblk.shape[:2]
    def gather(idx):
        Q, K = idx.shape[2:]
        bi = jnp.arange(B)[:, None, None]
        gi = jnp.arange(G)[None, :, None]
        g = sblk[bi, gi, idx.reshape(B, G, Q * K)]
        return g.reshape(B, G, Q, K * L_SEL, 2, HEAD_DIM)
    return gather


def paged_gatherer(pool, page_table, new_rows):
    DB, S = new_rows.shape[:2]
    bpp = PAGE_SIZE // L_SEL
    nb_past = PAST_LEN // L_SEL
    n_tail = -(-S // L_SEL)
    tail = jnp.pad(new_rows, ((0, 0), (0, n_tail * L_SEL - S), (0, 0), (0, 0), (0, 0)))
    tail = tail.reshape(DB, n_tail, L_SEL, 2, N_KV, HEAD_DIM)
    def gather(idx):
        Q, K = idx.shape[2:]
        bi = jnp.arange(DB)[:, None, None, None, None]
        gi = jnp.arange(N_KV)[None, :, None, None, None]
        r = jnp.arange(L_SEL)
        jp = jnp.minimum(idx, nb_past - 1)
        phys = page_table[bi[..., 0], jp // bpp]
        rows = (jp % bpp)[..., None] * L_SEL + r
        past = pool[phys[..., None], rows, :, gi, :]
        jt = jnp.clip(idx - nb_past, 0, n_tail - 1)
        tl = tail[bi, jt[..., None], r, :, gi, :]
        out = jnp.where((idx >= nb_past)[..., None, None, None], tl, past)
        return out.reshape(DB, N_KV, Q, K * L_SEL, 2, HEAD_DIM)
    return gather


def nsa_core(q, gate, qpos, ckv, c_end, ns, gather_sel, wkv, wpos):
    dt = q.dtype
    s = jnp.einsum('bqgjd,bngd->bgjqn', q, ckv[:, :, 0]).astype(jnp.float32)
    cmask = c_end[None, :] <= qpos[:, None]
    p_c = jax.nn.softmax(jnp.where(cmask, s, NEG_INF), axis=-1) * cmask
    o_c = jnp.einsum('bgjqn,bngd->bqgjd', p_c.astype(dt), ckv[:, :, 1])
    imp = cmp_to_sel(p_c.sum(axis=2), ns)
    blk = jnp.arange(ns)[None, :]
    cur = (qpos // L_SEL)[:, None]
    valid = blk <= cur
    forced = (blk == 0) | (blk == cur) | (blk == cur - 1)
    score = jnp.where(forced, FORCE_SCORE, jnp.where(valid, imp, -1.0))
    vals, idx = lax.top_k(score, min(N_SEL, ns))
    kv_sel = gather_sel(idx)
    kpos = idx[..., None] * L_SEL + jnp.arange(L_SEL)
    smask = ((vals >= 0)[..., None] & (kpos <= qpos[None, None, :, None, None])).reshape(idx.shape[:3] + (-1,))
    s = jnp.einsum('bqgjd,bgqkd->bgjqk', q, kv_sel[..., 0, :]).astype(jnp.float32)
    p_s = jax.nn.softmax(jnp.where(smask[:, :, None], s, NEG_INF), axis=-1)
    o_s = jnp.einsum('bgjqk,bgqkd->bqgjd', p_s.astype(dt), kv_sel[..., 1, :])
    s = jnp.einsum('bqgjd,bwgd->bgjqw', q, wkv[:, :, 0]).astype(jnp.float32)
    wmask = (wpos[None, :] >= 0) & (wpos[None, :] <= qpos[:, None]) & (wpos[None, :] >= qpos[:, None] - WINDOW)
    p_w = jax.nn.softmax(jnp.where(wmask, s, NEG_INF), axis=-1)
    o_w = jnp.einsum('bgjqw,bwgd->bqgjd', p_w.astype(dt), wkv[:, :, 1])
    return gate[..., 0:1] * o_c + gate[..., 1:2] * o_s + gate[..., 2:3] * o_w


def nsa_prompt_attend(q, gate, ckv, sblk, win_kv):
    B, T = q.shape[:2]
    ns = sblk.shape[2]
    c_end = cmp_end_positions(ckv)
    gather = block_gatherer(sblk)
    wpad = jnp.pad(win_kv, ((0, 0), (WINDOW, 0), (0, 0), (0, 0), (0, 0)))
    def step(i):
        s0 = i * NSA_QBLOCK
        qi = lax.dynamic_slice_in_dim(q, s0, NSA_QBLOCK, axis=1)
        gi = lax.dynamic_slice_in_dim(gate, s0, NSA_QBLOCK, axis=1)
        wi = lax.dynamic_slice_in_dim(wpad, s0, WINDOW + NSA_QBLOCK, axis=1)
        qpos = s0 + jnp.arange(NSA_QBLOCK)
        wpos = s0 - WINDOW + jnp.arange(WINDOW + NSA_QBLOCK)
        return nsa_core(qi, gi, qpos, ckv, c_end, ns, gather, wi, wpos)
    o = lax.map(step, jnp.arange(T // NSA_QBLOCK))
    return jnp.moveaxis(o, 0, 1).reshape(B, T, N_KV, HPG, HEAD_DIM)


def nsa_query(h, pos, g, w_qg):
    B, T, _ = h.shape
    a = rmsnorm(h, g) @ w_qg
    q = rope(a[..., :N_HEADS * HEAD_DIM].reshape(B, T, N_HEADS, HEAD_DIM), pos) * (HEAD_DIM ** -0.5)
    gate = jax.nn.sigmoid(a[..., N_HEADS * HEAD_DIM:].astype(jnp.float32)).astype(h.dtype)
    return q.reshape(B, T, N_KV, HPG, HEAD_DIM), gate.reshape(B, T, N_KV, HPG, N_BRANCH)


def setup_inputs(seed: int = 0) -> dict:
    key = jax.random.key(seed)
    ks = jax.random.split(key, 24)
    f32 = jnp.float32
    n_pages = PAST_LEN // PAGE_SIZE
    n_used = DEC_BATCH * n_pages
    n_pool = n_used + max(1, n_used // 4)
    win_buf = min(WINDOW, PAST_LEN)
    def nrm(k, shape, s=1.0):
        return jax.random.normal(k, shape, f32) * s
    page_table = jax.random.permutation(ks[6], n_pool)[:n_used].reshape(DEC_BATCH, n_pages).astype(jnp.int32)
    return {
        'x_prompt': nrm(ks[0], (BATCH, SEQ, D_MODEL)),
        'x_sample': nrm(ks[1], (DEC_BATCH, DEC_SEQ, D_MODEL)),
        'state_pool': nrm(ks[2], (N_A_LAYERS, DEC_BATCH, POOL_BUF, D_MODEL)),
        'cache_cmp_kv': nrm(ks[3], (n_pool, PAGE_SIZE, 2, N_KV, HEAD_DIM)),
        'cache_sel_kv': nrm(ks[4], (n_pool, PAGE_SIZE, 2, N_KV, HEAD_DIM)),
        'state_win_kv': nrm(ks[5], (DEC_BATCH, win_buf, 2, N_KV, HEAD_DIM)),
        'page_table': page_table,
        'norm_mix': 1.0 + nrm(ks[7], (DEPTH, D_MODEL), 0.05),
        'norm_ffn': 1.0 + nrm(ks[8], (DEPTH, D_MODEL), 0.05),
        'pool_w': nrm(ks[9], (N_A_LAYERS, N_POOL_GROUPS, POOL_GROUP, POOL_GROUP), POOL_GROUP ** -0.5),
        'pool_scale': 1.0 + nrm(ks[10], (N_A_LAYERS, D_MODEL), 0.1),
        'w_qg': nrm(ks[11], (N_B_LAYERS, D_MODEL, N_HEADS * HEAD_DIM + N_HEADS * N_BRANCH), D_MODEL ** -0.5),
        'w_o': nrm(ks[12], (N_B_LAYERS, N_HEADS * HEAD_DIM, D_MODEL), (N_HEADS * HEAD_DIM) ** -0.5),
        'norm_kv': 1.0 + nrm(ks[13], (D_MODEL,), 0.05),
        'w_kv': nrm(ks[14], (D_MODEL, N_BRANCH * 2 * N_KV * HEAD_DIM), D_MODEL ** -0.5),
        'cmp_pe': nrm(ks[15], (L_CMP, 2, HEAD_DIM), 0.1),
        'cmp_w1': nrm(ks[16], (2, L_CMP, HEAD_DIM, CMP_HIDDEN), (L_CMP * HEAD_DIM) ** -0.5),
        'cmp_w2': nrm(ks[17], (2, CMP_HIDDEN, HEAD_DIM), CMP_HIDDEN ** -0.5),
        'mlp_up': nrm(ks[18], (DEPTH, D_MODEL, D_FF), D_MODEL ** -0.5),
        'mlp_down': nrm(ks[19], (DEPTH, D_FF, D_MODEL), D_FF ** -0.5),
        'norm_final': 1.0 + nrm(ks[20], (D_MODEL,), 0.05),
    }


def reference(x_prompt, x_sample, state_pool, cache_cmp_kv, cache_sel_kv, state_win_kv, page_table,
              norm_mix, norm_ffn, pool_w, pool_scale, w_qg, w_o, norm_kv, w_kv, cmp_pe, cmp_w1, cmp_w2,
              mlp_up, mlp_down, norm_final):
    B, T, _ = x_prompt.shape
    DB, S, _ = x_sample.shape
    pos_p = jnp.arange(T)
    pos_s = PAST_LEN + jnp.arange(S)
    win_buf = state_win_kv.shape[1]
    hp, hs = x_prompt, x_sample
    pool_new_p, pool_new_s = [], []
    for l in range(DEPTH):
        if l < N_A_LAYERS:
            up = rmsnorm(hp, norm_mix[l])
            us = rmsnorm(hs, norm_mix[l])
            us_ext = jnp.concatenate([state_pool[l].astype(us.dtype), us], axis=1)
            hp = hp + pool_mix(up, 0, pool_w[l], pool_scale[l])
            hs = hs + pool_mix(us_ext, POOL_BUF, pool_w[l], pool_scale[l])
            pool_new_p.append(up[:, T - POOL_BUF:])
            pool_new_s.append(us_ext[:, S:])
        else:
            if l == N_A_LAYERS:
                cmp_p, sel_p, win_p = kv_side(hp, pos_p, norm_kv, w_kv)
                ckv_p = cmp_finish(cmp_parts(cmp_p, cmp_w1), cmp_pe, cmp_w1, cmp_w2)
                sblk_p = to_sel_blocks(sel_p, -(-T // L_SEL))
                cmp_s, sel_s, win_s = kv_side(hs, pos_s, norm_kv, w_kv)
                past_cmp = cache_cmp_kv[page_table].reshape(DB, PAST_LEN, 2, N_KV, HEAD_DIM)
                parts_s = jnp.concatenate([cmp_parts(past_cmp, cmp_w1), cmp_parts(cmp_s, cmp_w1)], axis=1)
                ckv_s = cmp_finish(parts_s, cmp_pe, cmp_w1, cmp_w2)
                c_end_s = cmp_end_positions(ckv_s)
                ns_s = PAST_LEN // L_SEL + -(-S // L_SEL)
                gather_s = paged_gatherer(cache_sel_kv, page_table, sel_s)
                win_ext = jnp.concatenate([state_win_kv.astype(win_s.dtype), win_s], axis=1)
                wpos_s = PAST_LEN - win_buf + jnp.arange(win_buf + S)
            b = l - N_A_LAYERS
            qp, gp = nsa_query(hp, pos_p, norm_mix[l], w_qg[b])
            op = nsa_prompt_attend(qp, gp, ckv_p, sblk_p, win_p)
            hp = hp + op.reshape(B, T, N_HEADS * HEAD_DIM) @ w_o[b]
            qs, gs = nsa_query(hs, pos_s, norm_mix[l], w_qg[b])
            o_s = nsa_core(qs, gs, pos_s, ckv_s, c_end_s, ns_s, gather_s, win_ext, wpos_s)
            hs = hs + o_s.reshape(DB, S, N_HEADS * HEAD_DIM) @ w_o[b]
        hp = hp + sq_relu_mlp(hp, norm_ffn[l], mlp_up[l], mlp_down[l])
        hs = hs + sq_relu_mlp(hs, norm_ffn[l], mlp_up[l], mlp_down[l])
    y_prompt = rmsnorm(hp, norm_final)
    y_sample = rmsnorm(hs, norm_final)
    cmp_kv_prompt = cmp_p
    sel_kv_prompt = sel_p
    win_kv_prompt = win_p[:, T - min(WINDOW, T):]
    pool_prompt = jnp.stack(pool_new_p)
    cmp_kv_sample = cmp_s
    sel_kv_sample = sel_s
    win_kv_sample = win_ext[:, S:]
    pool_sample = jnp.stack(pool_new_s)
    return (y_prompt, y_sample, cmp_kv_prompt, sel_kv_prompt, win_kv_prompt, pool_prompt, cmp_kv_sample, sel_kv_sample, win_kv_sample, pool_sample)
```

```python
import functools
import math

import numpy as np
import jax
import jax.numpy as jnp
from jax import lax
from jax.experimental import pallas as pl
from jax.experimental.pallas import tpu as pltpu

D_MODEL = 1024
POOL_WINDOWS = (2, 4, 8, 16)
POOL_GROUP = D_MODEL // len(POOL_WINDOWS)
POOL_BUF = max(POOL_WINDOWS) - 1
POOL_HALO = 16
N_HEADS = 16
N_KV = 4
HPG = N_HEADS // N_KV
HEAD_DIM = 64
KV_LANES = N_KV * HEAD_DIM
ROW_LANES = 2 * KV_LANES
ROT_DIM = HEAD_DIM // 4
ROPE_THETA = 500000.0
L_CMP = 32
STRIDE = 16
CMP_HIDDEN = 2 * HEAD_DIM
L_SEL = 64
SEL_RATIO = L_SEL // STRIDE
CMP_PARTS = L_CMP // STRIDE
N_SEL = 16
WINDOW = 512
N_BRANCH = 3
D_FF = 4 * D_MODEL
RMS_EPS = 1e-6
NEG_INF = -1e30
FORCE_SCORE = 1e9
PAGE_SIZE = 128
SEG_PER_PAGE = PAGE_SIZE // STRIDE
GATE_LANES = 128
LANE = 128

VMEM_LIMIT = 56 * 1024 * 1024

F32 = jnp.float32
BF16 = jnp.bfloat16


def _cparams(sem):
    return pltpu.CompilerParams(dimension_semantics=sem, vmem_limit_bytes=VMEM_LIMIT)


def _dot(a, b):
    return jnp.dot(a, b, preferred_element_type=F32)


def _dot_nt(a, b):
    return lax.dot_general(a, b, (((1,), (1,)), ((), ())), preferred_element_type=F32)


def _dot_exact_lhs(x, m_bf16):
    hi = x.astype(BF16)
    r1 = x - hi.astype(F32)
    mid = r1.astype(BF16)
    lo = (r1 - mid.astype(F32)).astype(BF16)
    return _dot(hi, m_bf16) + _dot(mid, m_bf16) + _dot(lo, m_bf16)


def _rms_scale(x):
    return x * lax.rsqrt(jnp.mean(x * x, axis=-1, keepdims=True) + RMS_EPS)


def _rope_lanes(x, c, sa, sb):
    outs = []
    for t in range(x.shape[1] // LANE):
        xc = x[:, t * LANE:(t + 1) * LANE]
        outs.append(xc * c + pltpu.roll(xc, LANE - ROT_DIM // 2, axis=1) * sa + pltpu.roll(xc, ROT_DIM // 2, axis=1) * sb)
    return jnp.concatenate(outs, axis=1)


def _pool_prompt_kernel(x_ref, xprev_ref, g_ref, w_ref, scale_ref, h_ref, buf_ref, *, tt):
    i = pl.program_id(1)
    x = x_ref[0]
    g = g_ref[...]
    u = _rms_scale(x) * g
    up = _rms_scale(xprev_ref[0]) * g
    up = jnp.where(i > 0, up, 0.0)
    ext = jnp.concatenate([up, u], axis=0)
    pos = i * tt + lax.broadcasted_iota(jnp.int32, (tt, 1), 0)
    scale = scale_ref[...]
    for gi, w in enumerate(POOL_WINDOWS):
        sl = slice(gi * POOL_GROUP, (gi + 1) * POOL_GROUP)
        s = ext[:, sl]
        k = 1
        while k < w:
            s = s + pltpu.roll(s, k, axis=0)
            k *= 2
        s = s[POOL_HALO:]
        cnt = jnp.minimum(pos + 1, w).astype(F32)
        d = s / cnt - u[:, sl]
        z = _dot(d.astype(BF16), w_ref[gi])
        h_ref[0, :, sl] = x[:, sl] + z * scale[:, sl]
    buf_ref[0] = u[tt - POOL_HALO:]


def _pool_prompt(x, g_mix, w_pool_bf16, scale, tt=512):
    B, T, D = x.shape
    hb = tt // POOL_HALO
    return pl.pallas_call(
        functools.partial(_pool_prompt_kernel, tt=tt),
        grid=(B, T // tt),
        in_specs=[
            pl.BlockSpec((1, tt, D), lambda b, i: (b, i, 0)),
            pl.BlockSpec((1, POOL_HALO, D), lambda b, i: (b, jnp.maximum(i * hb - 1, 0), 0)),
            pl.BlockSpec((1, D), lambda b, i: (0, 0)),
            pl.BlockSpec((len(POOL_WINDOWS), POOL_GROUP, POOL_GROUP), lambda b, i: (0, 0, 0)),
            pl.BlockSpec((1, D), lambda b, i: (0, 0)),
        ],
        out_specs=[
            pl.BlockSpec((1, tt, D), lambda b, i: (b, i, 0)),
            pl.BlockSpec((1, POOL_HALO, D), lambda b, i: (b, 0, 0)),
        ],
        out_shape=[jax.ShapeDtypeStruct((B, T, D), F32), jax.ShapeDtypeStruct((B, POOL_HALO, D), F32)],
        compiler_params=_cparams(("arbitrary", "arbitrary")),
        name="pool_prompt",
    )(x, x, g_mix, w_pool_bf16, scale)


def _pool_sample_kernel(x_ref, hist_ref, g_ref, w_ref, scale_ref, h_ref, u_ref):
    x = x_ref[...]
    u = _rms_scale(x) * g_ref[...]
    u_ref[...] = u
    scale = scale_ref[...]
    for gi, w in enumerate(POOL_WINDOWS):
        sl = slice(gi * POOL_GROUP, (gi + 1) * POOL_GROUP)
        s = u[:, sl]
        for r in range(1, w):
            s = s + hist_ref[POOL_BUF + 1 - r][:, sl]
        d = s / float(w) - u[:, sl]
        z = _dot(d.astype(BF16), w_ref[gi])
        h_ref[:, sl] = x[:, sl] + z * scale[:, sl]


def _pool_sample(x, hist, g_mix, w_pool_bf16, scale):
    DB, D = x.shape
    return pl.pallas_call(
        _pool_sample_kernel,
        out_shape=[jax.ShapeDtypeStruct((DB, D), F32), jax.ShapeDtypeStruct((DB, D), F32)],
        compiler_params=pltpu.CompilerParams(vmem_limit_bytes=VMEM_LIMIT),
        name="pool_sample",
    )(x, hist, g_mix, w_pool_bf16, scale)


def _mlp_kernel(*refs, has_attn, final_norm):
    if has_attn:
        h_ref, o_ref, wo_ref, g_ref, wup_ref, wdn_ref, gfin_ref, out_ref, hres, xn, acc = refs
    else:
        h_ref, g_ref, wup_ref, wdn_ref, gfin_ref, out_ref, hres, xn, acc = refs
    f = pl.program_id(1)

    @pl.when(f == 0)
    def _():
        h = h_ref[...]
        if has_attn:
            h = h + _dot(o_ref[...], wo_ref[...])
        hres[...] = h
        xn[...] = (_rms_scale(h) * g_ref[...]).astype(BF16)
        acc[...] = jnp.zeros_like(acc)

    a = jnp.maximum(_dot(xn[...], wup_ref[...]), 0.0)
    acc[...] += _dot((a * a).astype(BF16), wdn_ref[...])

    @pl.when(f == pl.num_programs(1) - 1)
    def _():
        y = hres[...] + acc[...]
        if final_norm:
            y = _rms_scale(y) * gfin_ref[...]
        out_ref[...] = y


def _mlp(h, g_ffn, w_up, w_down, g_final, o=None, w_o=None, final_norm=False, tm=1024, tf=512):
    M, D = h.shape
    tm = min(tm, M)
    has_attn = o is not None
    row = lambda i, f: (i, 0)
    const = lambda i, f: (0, 0)
    in_specs = [pl.BlockSpec((tm, D), row)]
    args = [h]
    if has_attn:
        in_specs += [pl.BlockSpec((tm, D), row), pl.BlockSpec((D, D), const)]
        args += [o, w_o]
    in_specs += [
        pl.BlockSpec((1, D), const),
        pl.BlockSpec((D, tf), lambda i, f: (0, f)),
        pl.BlockSpec((tf, D), lambda i, f: (f, 0)),
        pl.BlockSpec((1, D), const),
    ]
    args += [g_ffn, w_up, w_down, g_final]
    return pl.pallas_call(
        functools.partial(_mlp_kernel, has_attn=has_attn, final_norm=final_norm),
        grid=(M // tm, D_FF // tf),
        in_specs=in_specs,
        out_specs=pl.BlockSpec((tm, D), row),
        out_shape=jax.ShapeDtypeStruct((M, D), F32),
        scratch_shapes=[pltpu.VMEM((tm, D), F32), pltpu.VMEM((tm, D), BF16), pltpu.VMEM((tm, D), F32)],
        compiler_params=_cparams(("arbitrary", "arbitrary")),
        name="mlp",
    )(*args)


def _proj_kernel(h_ref, gkv_ref, gq_ref, wkv_ref, wq_ref, wg_ref, c_ref, sa_ref, sb_ref,
                 cmp_ref, sel_ref, win_ref, selb_ref, winb_ref, q_ref, gate_ref):
    y = _rms_scale(h_ref[...])
    xkv = (y * gkv_ref[...]).astype(BF16)
    xq = (y * gq_ref[...]).astype(BF16)
    c, sa, sb = c_ref[...], sa_ref[...], sb_ref[...]
    kv = _dot(xkv, wkv_ref[...])
    for br, (oref, bref) in enumerate(((cmp_ref, None), (sel_ref, selb_ref), (win_ref, winb_ref))):
        k = _rope_lanes(kv[:, br * ROW_LANES:br * ROW_LANES + KV_LANES], c, sa, sb)
        v = kv[:, br * ROW_LANES + KV_LANES:(br + 1) * ROW_LANES]
        oref[:, 0:KV_LANES] = k
        oref[:, KV_LANES:ROW_LANES] = v
        if bref is not None:
            bref[:, 0:KV_LANES] = k.astype(BF16)
            bref[:, KV_LANES:ROW_LANES] = v.astype(BF16)
    q = _rope_lanes(_dot(xq, wq_ref[...]), c, sa, sb) * (HEAD_DIM ** -0.5)
    q_ref[...] = q.astype(BF16)
    gate_ref[...] = jax.nn.sigmoid(_dot(xq, wg_ref[...]))


def _proj(h, g_kv, g_q, w_kv, w_q, w_g, rope_c, rope_sa, rope_sb, tt=512):
    M, D = h.shape
    tt = min(tt, M)
    nt = rope_c.shape[0] // tt
    row = lambda i: (i, 0)
    const = lambda i: (0, 0)
    tab = lambda i: (i % nt, 0)
    return pl.pallas_call(
        _proj_kernel,
        grid=(M // tt,),
        in_specs=[
            pl.BlockSpec((tt, D), row),
            pl.BlockSpec((1, D), const), pl.BlockSpec((1, D), const),
            pl.BlockSpec(w_kv.shape, const), pl.BlockSpec(w_q.shape, const), pl.BlockSpec(w_g.shape, const),
            pl.BlockSpec((tt, LANE), tab), pl.BlockSpec((tt, LANE), tab), pl.BlockSpec((tt, LANE), tab),
        ],
        out_specs=[pl.BlockSpec((tt, ROW_LANES), row)] * 5 + [pl.BlockSpec((tt, D), row), pl.BlockSpec((tt, GATE_LANES), row)],
        out_shape=[jax.ShapeDtypeStruct((M, ROW_LANES), F32)] * 3 + [jax.ShapeDtypeStruct((M, ROW_LANES), BF16)] * 2
                  + [jax.ShapeDtypeStruct((M, D), BF16), jax.ShapeDtypeStruct((M, GATE_LANES), F32)],
        compiler_params=_cparams(("arbitrary",)),
        name="kv_q_proj",
    )(h, g_kv, g_q, w_kv, w_q, w_g, rope_c, rope_sa, rope_sb)


def _parts_kernel(pages_ref, *refs, pg):
    del pages_ref
    x_refs, w1_ref, out_ref = refs[:pg], refs[pg], refs[pg + 1]
    for e in range(2):
        w = w1_ref[e]
        for g in range(N_KV):
            x = jnp.concatenate([xr[0, e, g] for xr in x_refs], axis=0)
            c0 = (e * N_KV + g) * 2 * CMP_HIDDEN
            out_ref[0, :, c0:c0 + 2 * CMP_HIDDEN] = _dot(x.astype(BF16), w)


def _cmp_parts(rows_t, pages, n_b, w1, pg=16):
    ppb = pages.shape[0] // n_b
    pg = min(pg, ppb)
    blk = (1, 2, N_KV, SEG_PER_PAGE, STRIDE * HEAD_DIM)

    def page_spec(p):
        return pl.BlockSpec(blk, lambda b, s, pages_ref: (pages_ref[b * ppb + s * pg + p], 0, 0, 0, 0))

    n_seg = ppb * SEG_PER_PAGE
    width = 2 * N_KV * 2 * CMP_HIDDEN
    grid_spec = pltpu.PrefetchScalarGridSpec(
        num_scalar_prefetch=1,
        grid=(n_b, ppb // pg),
        in_specs=[page_spec(p) for p in range(pg)] + [pl.BlockSpec(w1.shape, lambda b, s, pages_ref: (0, 0, 0))],
        out_specs=pl.BlockSpec((1, pg * SEG_PER_PAGE, width), lambda b, s, pages_ref: (b, s, 0)),
    )
    return pl.pallas_call(
        functools.partial(_parts_kernel, pg=pg),
        grid_spec=grid_spec,
        out_shape=jax.ShapeDtypeStruct((n_b, n_seg, width), F32),
        compiler_params=_cparams(("arbitrary", "arbitrary")),
        name="cmp_parts",
    )(pages, *([rows_t] * pg), w1)


def _cmp_finish_kernel(parts_ref, pe_ref, w1f_ref, w2_ref, out_ref, *, n_seg):
    rows = lax.broadcasted_iota(jnp.int32, (n_seg, 1), 0)
    for e in range(2):
        pe_sum = _dot(pe_ref[e], w1f_ref[e])[0:1]
        for gp in range(N_KV // 2):
            acts = []
            for g in (2 * gp, 2 * gp + 1):
                c0 = (e * N_KV + g) * 2 * CMP_HIDDEN
                first = parts_ref[0, :, c0:c0 + CMP_HIDDEN]
                second = pltpu.roll(parts_ref[0, :, c0 + CMP_HIDDEN:c0 + 2 * CMP_HIDDEN], n_seg - 1, axis=0)
                acts.append(jax.nn.gelu(first + second + pe_sum))
            a = jnp.concatenate(acts, axis=1).astype(BF16)
            o = _dot(a, w2_ref[e])
            o = jnp.where(rows < n_seg - 1, o, 0.0)
            out_ref[0, :, e * KV_LANES + gp * 2 * HEAD_DIM:e * KV_LANES + (gp + 1) * 2 * HEAD_DIM] = o.astype(BF16)


def _cmp_finish(parts, pe8, w1f, w2bd):
    n_b, n_seg, width = parts.shape
    return pl.pallas_call(
        functools.partial(_cmp_finish_kernel, n_seg=n_seg),
        grid=(n_b,),
        in_specs=[
            pl.BlockSpec((1, n_seg, width), lambda b: (b, 0, 0)),
            pl.BlockSpec(pe8.shape, lambda b: (0, 0, 0)),
            pl.BlockSpec(w1f.shape, lambda b: (0, 0, 0)),
            pl.BlockSpec(w2bd.shape, lambda b: (0, 0, 0)),
        ],
        out_specs=pl.BlockSpec((1, n_seg, ROW_LANES), lambda b: (b, 0, 0)),
        out_shape=jax.ShapeDtypeStruct((n_b, n_seg, ROW_LANES), BF16),
        compiler_params=_cparams(("arbitrary",)),
        name="cmp_finish",
    )(parts, pe8, w1f, w2bd)


def _masked_softmax(s, mask):
    sm = jnp.where(mask, s, NEG_INF)
    m = jnp.max(sm, axis=-1, keepdims=True)
    e = jnp.where(mask, jnp.exp(sm - m), 0.0)
    l = jnp.sum(e, axis=-1, keepdims=True)
    return e / jnp.where(l > 0.0, l, 1.0)


def _topk_mask(score, valid, n_lanes):
    s2 = jnp.concatenate([score, score], axis=1)
    lane = lax.broadcasted_iota(jnp.int32, (1, 2 * n_lanes), 1) % n_lanes
    rank = jnp.zeros(s2.shape, F32)
    for d in range(1, n_lanes):
        other = pltpu.roll(s2, d, axis=1)
        tie = jnp.where(lane >= d, 1.0, 0.0)
        rank = rank + jnp.where(other > s2, 1.0, jnp.where(other == s2, tie, 0.0))
    return (rank[:, :n_lanes] < float(N_SEL)) & valid


def _attn_prompt_kernel(q_ref, gate_ref, ckv_ref, sel_ref, win_ref, imp_ref, exp_ref, o_ref, *, tq, kc, t_len):
    i = pl.program_id(1)
    s0 = i * tq
    q = q_ref[0]
    gate = gate_ref[0]
    n_cmp = ckv_ref.shape[1]
    lane_kv = lax.broadcasted_iota(jnp.int32, (1, KV_LANES), 1)
    qpos = s0 + lax.broadcasted_iota(jnp.int32, (tq, 1), 0)
    qpos4 = jnp.concatenate([qpos] * HPG, axis=0)
    gate_lane = lax.broadcasted_iota(jnp.int32, (1, GATE_LANES), 1)
    n_blk = t_len // L_SEL
    blk = lax.broadcasted_iota(jnp.int32, (1, n_blk), 1)
    cur = qpos // L_SEL
    valid = blk <= cur
    forced = (blk == 0) | (blk == cur) | (blk == cur - 1)
    c_end = lax.broadcasted_iota(jnp.int32, (1, n_cmp), 1) * STRIDE + (L_CMP - 1)
    cmask = c_end <= qpos4
    n_chunk = (s0 + tq + kc - 1) // kc
    w_len = WINDOW + tq
    w0 = pl.multiple_of(jnp.clip(s0 - WINDOW, 0, t_len - w_len), tq)
    wpos = w0 + lax.broadcasted_iota(jnp.int32, (1, w_len), 1)
    wmask = (wpos <= qpos4) & (wpos >= qpos4 - WINDOW)
    out = [jnp.zeros((tq, KV_LANES), F32) for _ in range(HPG)]

    for g in range(N_KV):
        lm = (lane_kv // HEAD_DIM) == g
        qe = jnp.concatenate(
            [jnp.where(lm, q[:, j * KV_LANES:(j + 1) * KV_LANES], jnp.zeros((), BF16)) for j in range(HPG)], axis=0)

        p_c = _masked_softmax(_dot_nt(qe, ckv_ref[0, :, 0:KV_LANES]), cmask)
        o_c = _dot(p_c.astype(BF16), ckv_ref[0, :, KV_LANES:ROW_LANES])
        p_sum = p_c[0:tq]
        for j in range(1, HPG):
            p_sum = p_sum + p_c[j * tq:(j + 1) * tq]
        imp = _dot_exact_lhs(p_sum, imp_ref[...])
        score = jnp.where(forced, FORCE_SCORE, jnp.where(valid, imp, -1.0))
        selb = jnp.where(_topk_mask(score, valid, n_blk), 1.0, 0.0).astype(BF16)

        def chunk(c, carry):
            m_i, l_i, acc = carry
            k0 = pl.multiple_of(c * kc, kc)
            s = _dot_nt(qe, sel_ref[0, pl.ds(k0, kc), 0:KV_LANES])
            mk = _dot(selb, exp_ref[c])
            mk4 = jnp.concatenate([mk] * HPG, axis=0)
            kpos = k0 + lax.broadcasted_iota(jnp.int32, (1, kc), 1)
            msk = (mk4 > 0.5) & (kpos <= qpos4)
            sm = jnp.where(msk, s, NEG_INF)
            m_new = jnp.maximum(m_i, jnp.max(sm, axis=-1, keepdims=True))
            alpha = jnp.exp(m_i - m_new)
            e = jnp.where(msk, jnp.exp(sm - m_new), 0.0)
            l_new = alpha * l_i + jnp.sum(e, axis=-1, keepdims=True)
            acc = alpha * acc + _dot(e.astype(BF16), sel_ref[0, pl.ds(k0, kc), KV_LANES:ROW_LANES])
            return m_new, l_new, acc

        init = (jnp.full((HPG * tq, 1), NEG_INF, F32), jnp.zeros((HPG * tq, 1), F32), jnp.zeros((HPG * tq, KV_LANES), F32))
        _, l_s, acc_s = lax.fori_loop(0, n_chunk, chunk, init)
        o_s = acc_s / l_s

        p_w = _masked_softmax(_dot_nt(qe, win_ref[0, pl.ds(w0, w_len), 0:KV_LANES]), wmask)
        o_w = _dot(p_w.astype(BF16), win_ref[0, pl.ds(w0, w_len), KV_LANES:ROW_LANES])

        for j in range(HPG):
            rs = slice(j * tq, (j + 1) * tq)
            col = (g * HPG + j) * N_BRANCH
            gc = jnp.sum(jnp.where(gate_lane == col, gate, 0.0), axis=-1, keepdims=True)
            gs = jnp.sum(jnp.where(gate_lane == col + 1, gate, 0.0), axis=-1, keepdims=True)
            gw = jnp.sum(jnp.where(gate_lane == col + 2, gate, 0.0), axis=-1, keepdims=True)
            comb = gc * o_c[rs] + gs * o_s[rs] + gw * o_w[rs]
            out[j] = out[j] + jnp.where(lm, comb, 0.0)

    o_ref[0] = jnp.concatenate(out, axis=1).astype(BF16)


def _attn_prompt(q, gate, ckv, selb, winb, imp_mat, expand, tq=128, kc=512):
    B, T, D = q.shape
    n_cmp = ckv.shape[1]
    return pl.pallas_call(
        functools.partial(_attn_prompt_kernel, tq=tq, kc=kc, t_len=T),
        grid=(B, T // tq),
        in_specs=[
            pl.BlockSpec((1, tq, D), lambda b, i: (b, i, 0)),
            pl.BlockSpec((1, tq, GATE_LANES), lambda b, i: (b, i, 0)),
            pl.BlockSpec((1, n_cmp, ROW_LANES), lambda b, i: (b, 0, 0)),
            pl.BlockSpec((1, T, ROW_LANES), lambda b, i: (b, 0, 0)),
            pl.BlockSpec((1, T, ROW_LANES), lambda b, i: (b, 0, 0)),
            pl.BlockSpec(imp_mat.shape, lambda b, i: (0, 0)),
            pl.BlockSpec(expand.shape, lambda b, i: (0, 0, 0)),
        ],
        out_specs=pl.BlockSpec((1, tq, D), lambda b, i: (b, i, 0)),
        out_shape=jax.ShapeDtypeStruct((B, T, D), BF16),
        compiler_params=_cparams(("arbitrary", "arbitrary")),
        name="attn_prompt",
    )(q, gate, ckv, selb, winb, imp_mat, expand)


def _query_rows(q_row):
    q4 = jnp.concatenate([q_row[:, j * KV_LANES:(j + 1) * KV_LANES] for j in range(HPG)], axis=0)
    q16 = jnp.concatenate([q4] * N_KV, axis=0)
    rg = lax.broadcasted_iota(jnp.int32, (N_HEADS, 1), 0) // HPG
    lg = lax.broadcasted_iota(jnp.int32, (1, KV_LANES), 1) // HEAD_DIM
    return jnp.where(rg == lg, q16, 0.0).astype(BF16)


def _sample_cmp_kernel(q_ref, ckv_ref, imp_ref, oc_ref, idx_ref, val_ref, *, n_valid, cur, n_blk_lanes):
    qe = _query_rows(q_ref[0])
    n_cmp = ckv_ref.shape[1]
    cmask = lax.broadcasted_iota(jnp.int32, (1, n_cmp), 1) < n_valid
    p = _masked_softmax(_dot_nt(qe, ckv_ref[0, :, 0:KV_LANES]), cmask)
    oc_ref[0] = _dot(p.astype(BF16), ckv_ref[0, :, KV_LANES:ROW_LANES])
    p_sum = jnp.concatenate([jnp.sum(p[g * HPG:(g + 1) * HPG], axis=0, keepdims=True) for g in range(N_KV)], axis=0)
    imp = _dot_exact_lhs(p_sum, imp_ref[...])
    blk = lax.broadcasted_iota(jnp.int32, (1, n_blk_lanes), 1)
    blk_f = blk.astype(F32)
    valid = blk <= cur
    forced = (blk == 0) | (blk == cur) | (blk == cur - 1)
    score = jnp.where(forced, FORCE_SCORE, jnp.where(valid, imp, -1.0))
    score = jnp.where(valid, score, -2.0)
    out_lane = lax.broadcasted_iota(jnp.int32, (1, LANE), 1)
    idx_acc = jnp.zeros((N_KV, LANE), F32)
    val_acc = jnp.full((N_KV, LANE), -1.0, F32)
    for t in range(N_SEL):
        mx = jnp.max(score, axis=-1, keepdims=True)
        ix = jnp.min(jnp.where(score == mx, blk_f, float(n_blk_lanes)), axis=-1, keepdims=True)
        idx_acc = jnp.where(out_lane == t, ix, idx_acc)
        val_acc = jnp.where(out_lane == t, mx, val_acc)
        score = jnp.where(blk_f == ix, -3.0, score)
    idx_ref[0] = idx_acc.astype(jnp.int32)
    val_ref[0] = val_acc


def _sample_cmp(q, ckv, imp_mat, n_valid, cur):
    DB = q.shape[0]
    n_cmp = ckv.shape[1]
    nbl = imp_mat.shape[1]
    return pl.pallas_call(
        functools.partial(_sample_cmp_kernel, n_valid=n_valid, cur=cur, n_blk_lanes=nbl),
        grid=(DB,),
        in_specs=[
            pl.BlockSpec((1, 1, D_MODEL), lambda b: (b, 0, 0)),
            pl.BlockSpec((1, n_cmp, ROW_LANES), lambda b: (b, 0, 0)),
            pl.BlockSpec(imp_mat.shape, lambda b: (0, 0)),
        ],
        out_specs=[
            pl.BlockSpec((1, N_HEADS, KV_LANES), lambda b: (b, 0, 0)),
            pl.BlockSpec((1, N_KV, LANE), lambda b: (b, 0, 0)),
            pl.BlockSpec((1, N_KV, LANE), lambda b: (b, 0, 0)),
        ],
        out_shape=[jax.ShapeDtypeStruct((DB, N_HEADS, KV_LANES), F32),
                   jax.ShapeDtypeStruct((DB, N_KV, LANE), jnp.int32),
                   jax.ShapeDtypeStruct((DB, N_KV, LANE), F32)],
        compiler_params=_cparams(("arbitrary",)),
        name="sample_cmp_topk",
    )(q, ckv, imp_mat)


def _sample_sel_kernel(hp_ref, idx_ref, ok_ref, b0_ref, b1_ref, b2_ref, b3_ref, q_ref, gate_ref, selnew_ref,
                       win_ref, winnew_ref, oc_ref, o_ref, m_sc, l_sc, acc_sc, *, qpos, nb_past):
    del hp_ref
    b = pl.program_id(0)
    k = pl.program_id(1)
    qe = _query_rows(q_ref[0])
    row_g = lax.broadcasted_iota(jnp.int32, (N_HEADS, 1), 0) // HPG
    lane_g = lax.broadcasted_iota(jnp.int32, (1, KV_LANES), 1) // HEAD_DIM
    own = row_g == lane_g

    @pl.when(k == 0)
    def _():
        m_sc[...] = jnp.full(m_sc.shape, NEG_INF, F32)
        l_sc[...] = jnp.zeros_like(l_sc)
        acc_sc[...] = jnp.zeros_like(acc_sc)

    row_in_blk = lax.broadcasted_iota(jnp.int32, (L_SEL, 1), 0)
    new_row = selnew_ref[0]
    s = jnp.zeros((N_HEADS, L_SEL), F32)
    keep = jnp.zeros((N_HEADS, L_SEL), F32)
    vals = []
    for g, bref in enumerate((b0_ref, b1_ref, b2_ref, b3_ref)):
        n = (b * N_KV + g) * N_SEL + k
        idx = idx_ref[n]
        first_of_tail = (row_in_blk + jnp.where(idx >= nb_past, 0, L_SEL)) == 0
        rows = jnp.where(first_of_tail, new_row, bref[0])
        vals.append(rows[:, KV_LANES:ROW_LANES].astype(BF16))
        sg = _dot_nt(qe, rows[:, 0:KV_LANES].astype(BF16))
        kpos = idx * L_SEL + lax.broadcasted_iota(jnp.int32, (1, L_SEL), 1)
        kpos = kpos + jnp.where(ok_ref[n] > 0, 0, qpos + 1)
        mine = row_g == g
        s = jnp.where(mine, sg, s)
        keep = jnp.where(mine, jnp.where(kpos <= qpos, 1.0, 0.0), keep)
    msk = keep > 0.5
    sm = jnp.where(msk, s, NEG_INF)
    m_new = jnp.maximum(m_sc[...], jnp.max(sm, axis=-1, keepdims=True))
    alpha = jnp.exp(m_sc[...] - m_new)
    e = jnp.where(msk, jnp.exp(sm - m_new), 0.0)
    l_sc[...] = alpha * l_sc[...] + jnp.sum(e, axis=-1, keepdims=True)
    eb = e.astype(BF16)
    pv = jnp.zeros((N_HEADS, KV_LANES), F32)
    for g in range(N_KV):
        pv = jnp.where(row_g == g, _dot(eb, vals[g]), pv)
    acc_sc[...] = alpha * acc_sc[...] + pv
    m_sc[...] = m_new

    @pl.when(k == pl.num_programs(1) - 1)
    def _():
        o_s = acc_sc[...] / l_sc[...]
        n_win = win_ref.shape[1]
        wrows = jnp.concatenate([win_ref[0], jnp.zeros((LANE, ROW_LANES), F32)], axis=0)
        wrow_id = lax.broadcasted_iota(jnp.int32, (n_win + LANE, 1), 0)
        wrows = jnp.where(wrow_id == n_win, winnew_ref[0], wrows)
        wpos = qpos - n_win + lax.broadcasted_iota(jnp.int32, (1, n_win + LANE), 1)
        wmask = (wpos >= 0) & (wpos <= qpos) & (wpos >= qpos - WINDOW)
        p_w = _masked_softmax(_dot_nt(qe, wrows[:, 0:KV_LANES].astype(BF16)), wmask)
        o_w = _dot(p_w.astype(BF16), wrows[:, KV_LANES:ROW_LANES].astype(BF16))
        gate = gate_ref[0]
        gl = lax.broadcasted_iota(jnp.int32, (1, GATE_LANES), 1)
        rr = lax.broadcasted_iota(jnp.int32, (N_HEADS, 1), 0) * N_BRANCH
        gb = jnp.broadcast_to(gate, (N_HEADS, GATE_LANES))
        gc = jnp.sum(jnp.where(gl == rr, gb, 0.0), axis=-1, keepdims=True)
        gs = jnp.sum(jnp.where(gl == rr + 1, gb, 0.0), axis=-1, keepdims=True)
        gw = jnp.sum(jnp.where(gl == rr + 2, gb, 0.0), axis=-1, keepdims=True)
        comb = jnp.where(own, gc * oc_ref[0] + gs * o_s + gw * o_w, 0.0)
        pieces = []
        for j in range(HPG):
            acc = comb[j:j + 1]
            for g in range(1, N_KV):
                acc = acc + comb[g * HPG + j:g * HPG + j + 1]
            pieces.append(acc)
        o_ref[0] = jnp.concatenate(pieces, axis=1)


def _sample_sel(hp, idx, ok, cache_sel_blocks, q, gate, sel_new, win_state, win_new, o_c, qpos, nb_past):
    DB = q.shape[0]
    n_win = win_state.shape[1]

    def blk_spec(g):
        return pl.BlockSpec((1, L_SEL, ROW_LANES),
                            lambda b, k, hp_ref, idx_ref, ok_ref: (hp_ref[(b * N_KV + g) * N_SEL + k], 0, 0))

    per_b = lambda b, k, *_: (b, 0, 0)
    grid_spec = pltpu.PrefetchScalarGridSpec(
        num_scalar_prefetch=3,
        grid=(DB, N_SEL),
        in_specs=[blk_spec(g) for g in range(N_KV)] + [
            pl.BlockSpec((1, 1, D_MODEL), per_b),
            pl.BlockSpec((1, 1, GATE_LANES), per_b),
            pl.BlockSpec((1, 1, ROW_LANES), per_b),
            pl.BlockSpec((1, n_win, ROW_LANES), per_b),
            pl.BlockSpec((1, 1, ROW_LANES), per_b),
            pl.BlockSpec((1, N_HEADS, KV_LANES), per_b),
        ],
        out_specs=pl.BlockSpec((1, 1, D_MODEL), per_b),
        scratch_shapes=[pltpu.VMEM((N_HEADS, 1), F32), pltpu.VMEM((N_HEADS, 1), F32), pltpu.VMEM((N_HEADS, KV_LANES), F32)],
    )
    return pl.pallas_call(
        functools.partial(_sample_sel_kernel, qpos=qpos, nb_past=nb_past),
        grid_spec=grid_spec,
        out_shape=jax.ShapeDtypeStruct((DB, 1, D_MODEL), F32),
        compiler_params=_cparams(("arbitrary", "arbitrary")),
        name="sample_sel_win",
    )(hp, idx, ok, *([cache_sel_blocks] * N_KV), q, gate, sel_new, win_state, win_new, o_c)


def _head_perm():
    idx = np.arange(N_HEADS * HEAD_DIM).reshape(N_KV, HPG, HEAD_DIM)
    return idx.transpose(1, 0, 2).reshape(-1)


def _imp_matrix(n_cmp_rows, n_blk_cols):
    m = np.zeros((n_cmp_rows, n_blk_cols), np.float32)
    for j in range(n_blk_cols):
        for a in range(SEL_RATIO):
            for c in range(CMP_PARTS):
                i = SEL_RATIO * j + a - c
                if 0 <= i < n_cmp_rows:
                    m[i, j] += 1.0
    return jnp.asarray(m, BF16)


def _expand_matrix(n_blk, kc):
    t = n_blk * L_SEL
    key_blk = np.arange(t) // L_SEL
    e = (key_blk[None, :] == np.arange(n_blk)[:, None]).astype(np.float32)
    return jnp.asarray(e.reshape(n_blk, t // kc, kc).transpose(1, 0, 2), BF16)


def _rope_tables(pos):
    half = ROT_DIM // 2
    inv = ROPE_THETA ** (-jnp.arange(0, ROT_DIM, 2, dtype=F32) / ROT_DIM)
    ang = pos.astype(F32)[:, None] * inv[None, :]
    cos, sin = jnp.cos(ang), jnp.sin(ang)
    ones = jnp.ones((pos.shape[0], HEAD_DIM - ROT_DIM), F32)
    zeros = jnp.zeros((pos.shape[0], HEAD_DIM - ROT_DIM), F32)
    zh = jnp.zeros_like(sin)
    c = jnp.concatenate([cos, cos, ones], axis=1)
    sa = jnp.concatenate([-sin, zh, zeros], axis=1)
    sb = jnp.concatenate([zh, sin, zeros], axis=1)
    rep = LANE // HEAD_DIM
    return tuple(jnp.tile(t, (1, rep)) for t in (c, sa, sb))


def _segment_major(rows, n_pages):
    x = rows.reshape(n_pages, SEG_PER_PAGE, STRIDE, 2, N_KV, HEAD_DIM)
    return x.transpose(0, 3, 4, 1, 2, 5).reshape(n_pages, 2, N_KV, SEG_PER_PAGE, STRIDE * HEAD_DIM)


def kernel(x_prompt, x_sample, state_pool, cache_cmp_kv, cache_sel_kv, state_win_kv, page_table,
           norm_mix, norm_ffn, pool_w, pool_scale, w_qg, w_o, norm_kv, w_kv, cmp_pe, cmp_w1, cmp_w2,
           mlp_up, mlp_down, norm_final):
    B, T, D = x_prompt.shape
    DB, S, _ = x_sample.shape
    n_pool = cache_cmp_kv.shape[0]
    past_len = page_table.shape[1] * PAGE_SIZE
    win_buf = state_win_kv.shape[1]
    assert S == 1 and D == D_MODEL and T % PAGE_SIZE == 0

    perm = _head_perm()
    n_q = N_HEADS * HEAD_DIM
    w_q = w_qg[0][:, :n_q][:, perm].astype(BF16)
    w_g = jnp.pad(w_qg[0][:, n_q:], ((0, 0), (0, GATE_LANES - N_HEADS * N_BRANCH))).astype(BF16)
    w_o_p = w_o[0][perm, :].astype(BF16)
    w_kv_b = w_kv.astype(BF16)
    pool_w_b = pool_w[0].astype(BF16)
    up_b, down_b = mlp_up.astype(BF16), mlp_down.astype(BF16)
    g_mix0, g_mix1 = norm_mix[0:1], norm_mix[1:2]
    g_kv, g_fin = norm_kv[None, :], norm_final[None, :]
    w1 = cmp_w1.reshape(2, CMP_PARTS, STRIDE, HEAD_DIM, CMP_HIDDEN).transpose(0, 2, 3, 1, 4)
    w1 = w1.reshape(2, STRIDE * HEAD_DIM, CMP_PARTS * CMP_HIDDEN).astype(BF16)
    w1_flat = cmp_w1.reshape(2, L_CMP * HEAD_DIM, CMP_HIDDEN).astype(BF16)
    pe8 = jnp.broadcast_to(cmp_pe.transpose(1, 0, 2).reshape(2, 1, L_CMP * HEAD_DIM), (2, 8, L_CMP * HEAD_DIM)).astype(BF16)
    zero_w2 = jnp.zeros_like(cmp_w2)
    w2bd = jnp.concatenate([jnp.concatenate([cmp_w2, zero_w2], axis=2),
                            jnp.concatenate([zero_w2, cmp_w2], axis=2)], axis=1).astype(BF16)

    h1, pool_tail = _pool_prompt(x_prompt, g_mix0, pool_w_b, pool_scale)
    pool_prompt = pool_tail[None, :, POOL_HALO - POOL_BUF:]
    h2 = _mlp(h1.reshape(B * T, D), norm_ffn[0:1], up_b[0], down_b[0], g_fin)
    rope_p = _rope_tables(jnp.arange(T))
    cmp_p, sel_p, win_p, selb_p, winb_p, q_p, gate_p = _proj(h2, g_kv, g_mix1, w_kv_b, w_q, w_g, *rope_p)
    n_pages_p = B * T // PAGE_SIZE
    parts_p = _cmp_parts(_segment_major(cmp_p, n_pages_p), jnp.arange(n_pages_p, dtype=jnp.int32), B, w1)
    ckv_p = _cmp_finish(parts_p, pe8, w1_flat, w2bd)
    n_blk_p = T // L_SEL
    kc = 512
    o_p = _attn_prompt(q_p.reshape(B, T, D), gate_p.reshape(B, T, GATE_LANES), ckv_p,
                       selb_p.reshape(B, T, ROW_LANES), winb_p.reshape(B, T, ROW_LANES),
                       _imp_matrix(ckv_p.shape[1], n_blk_p), _expand_matrix(n_blk_p, kc), kc=kc)
    y_p = _mlp(h2, norm_ffn[1:2], up_b[1], down_b[1], g_fin, o=o_p.reshape(B * T, D), w_o=w_o_p, final_norm=True)

    hist = jnp.concatenate([jnp.zeros((1, DB, D), F32), state_pool[0].transpose(1, 0, 2)], axis=0)
    hs1, us = _pool_sample(x_sample[:, 0], hist, g_mix0, pool_w_b, pool_scale)
    pool_sample = jnp.concatenate([state_pool[0][:, 1:], us[:, None]], axis=1)[None]
    hs2 = _mlp(hs1, norm_ffn[0:1], up_b[0], down_b[0], g_fin)
    rope_s = _rope_tables(jnp.full((DB,), past_len, jnp.int32))
    cmp_s, sel_s, win_s, _, _, q_s, gate_s = _proj(hs2, g_kv, g_mix1, w_kv_b, w_q, w_g, *rope_s)
    parts_s = _cmp_parts(_segment_major(cache_cmp_kv, n_pool), page_table.reshape(-1), DB, w1)
    ckv_s = _cmp_finish(parts_s, pe8, w1_flat, w2bd)
    n_valid = past_len // STRIDE - CMP_PARTS + 1
    nb_past = past_len // L_SEL
    cur = past_len // L_SEL
    n_blk_lanes = -(-(nb_past + 1) // LANE) * LANE
    q_s3 = q_s.astype(F32)[:, None]
    o_c, idx_pad, val_pad = _sample_cmp(q_s3, ckv_s, _imp_matrix(ckv_s.shape[1], n_blk_lanes), n_valid, cur)
    idx = idx_pad[:, :, :N_SEL]
    ok = (val_pad[:, :, :N_SEL] >= 0).astype(jnp.int32)
    bpp = PAGE_SIZE // L_SEL
    jp = jnp.minimum(idx, nb_past - 1)
    phys = jnp.take_along_axis(page_table[:, None, :], (jp // bpp).reshape(DB, 1, -1), axis=2).reshape(idx.shape)
    hp = phys * bpp + jp % bpp
    sel_blocks = cache_sel_kv.reshape(n_pool * bpp, L_SEL, ROW_LANES)
    o_s = _sample_sel(hp.reshape(-1), idx.reshape(-1), ok.reshape(-1), sel_blocks, q_s3, gate_s[:, None],
                      sel_s[:, None], state_win_kv.reshape(DB, win_buf, ROW_LANES), win_s[:, None], o_c,
                      past_len, nb_past)
    y_s = _mlp(hs2, norm_ffn[1:2], up_b[1], down_b[1], g_fin, o=o_s[:, 0].astype(BF16), w_o=w_o_p, final_norm=True)

    kv5 = lambda a, n, t: a.reshape(n, t, 2, N_KV, HEAD_DIM)
    win_sample = jnp.concatenate([state_win_kv, kv5(win_s, DB, 1)], axis=1)[:, S:]
    return (y_p.reshape(B, T, D), y_s.reshape(DB, S, D),
            kv5(cmp_p, B, T), kv5(sel_p, B, T), kv5(win_p, B, T)[:, T - min(WINDOW, T):], pool_prompt,
            kv5(cmp_s, DB, 1), kv5(sel_s, DB, 1), win_sample, pool_sample)
```

```python
import functools

import numpy as np
import jax
import jax.numpy as jnp
from jax import lax
from jax.experimental import pallas as pl
from jax.experimental.pallas import tpu as pltpu

D_MODEL = 1024
POOL_WINDOWS = (2, 4, 8, 16)
POOL_GROUP = D_MODEL // len(POOL_WINDOWS)
POOL_BUF = max(POOL_WINDOWS) - 1
POOL_HALO = 16
N_HEADS = 16
N_KV = 4
HPG = N_HEADS // N_KV
HEAD_DIM = 64
KV_LANES = N_KV * HEAD_DIM
ROW_LANES = 2 * KV_LANES
ROT_DIM = HEAD_DIM // 4
ROT_HALF = ROT_DIM // 2
ROPE_THETA = 500000.0
L_CMP = 32
STRIDE = 16
CMP_HIDDEN = 2 * HEAD_DIM
L_SEL = 64
SEL_RATIO = L_SEL // STRIDE
CMP_PARTS = L_CMP // STRIDE
N_SEL = 16
WINDOW = 512
N_BRANCH = 3
D_FF = 4 * D_MODEL
RMS_EPS = 1e-6
NEG_INF = -1e30
FORCE_SCORE = 1e9
PAGE_SIZE = 128
SEG_PER_PAGE = PAGE_SIZE // STRIDE
GATE_LANES = 128
LANE = 128
SUBLANE = 8
HEAD_SLOT = LANE
JPAD = SUBLANE

VMEM_LIMIT = 56 * 1024 * 1024

F32 = jnp.float32
BF16 = jnp.bfloat16


def _cparams(sem):
    return pltpu.CompilerParams(dimension_semantics=sem, vmem_limit_bytes=VMEM_LIMIT)


def _dot(a, b):
    return jnp.dot(a, b, preferred_element_type=F32)


def _dot_nt(a, b):
    return lax.dot_general(a, b, (((1,), (1,)), ((), ())), preferred_element_type=F32)


def _split3(x):
    hi = x.astype(BF16)
    r1 = x - hi.astype(F32)
    mid = r1.astype(BF16)
    lo = (r1 - mid.astype(F32)).astype(BF16)
    return hi, mid, lo


def _dot_exact_lhs(x, m_bf16):
    hi, mid, lo = _split3(x)
    return _dot(hi, m_bf16) + _dot(mid, m_bf16) + _dot(lo, m_bf16)


def _dot_exact_rhs(m_bf16, x):
    hi, mid, lo = _split3(x)
    return _dot(m_bf16, hi) + _dot(m_bf16, mid) + _dot(m_bf16, lo)


def _rms_scale(x):
    return x * lax.rsqrt(jnp.mean(x * x, axis=-1, keepdims=True) + RMS_EPS)


def _rope_lanes(x, c, sa, sb):
    outs = []
    for t in range(x.shape[1] // LANE):
        xc = x[:, t * LANE:(t + 1) * LANE]
        outs.append(xc * c + pltpu.roll(xc, LANE - ROT_HALF, axis=1) * sa + pltpu.roll(xc, ROT_HALF, axis=1) * sb)
    return jnp.concatenate(outs, axis=1)


def _masked_softmax(s, mask, axis):
    sm = jnp.where(mask, s, NEG_INF)
    m = jnp.max(sm, axis=axis, keepdims=True)
    e = jnp.where(mask, jnp.exp(sm - m), 0.0)
    l = jnp.sum(e, axis=axis, keepdims=True)
    return e / jnp.where(l > 0.0, l, 1.0)


def _pool_prompt_kernel(x_ref, xprev_ref, g_ref, w_ref, scale_ref, h_ref, buf_ref, *, tt):
    i = pl.program_id(1)
    x = x_ref[0]
    g = g_ref[...]
    u = _rms_scale(x) * g
    up = _rms_scale(xprev_ref[0]) * g
    up = jnp.where(i > 0, up, 0.0)
    ext = jnp.concatenate([up, u], axis=0)
    pos = i * tt + lax.broadcasted_iota(jnp.int32, (tt, 1), 0)
    scale = scale_ref[...]
    for gi, w in enumerate(POOL_WINDOWS):
        sl = slice(gi * POOL_GROUP, (gi + 1) * POOL_GROUP)
        s = ext[:, sl]
        k = 1
        while k < w:
            s = s + pltpu.roll(s, k, axis=0)
            k *= 2
        s = s[POOL_HALO:]
        cnt = jnp.minimum(pos + 1, w).astype(F32)
        d = s / cnt - u[:, sl]
        z = _dot(d.astype(BF16), w_ref[gi])
        h_ref[0, :, sl] = x[:, sl] + z * scale[:, sl]
    buf_ref[0] = u[tt - POOL_HALO:]


def _pool_prompt(x, g_mix, w_pool_bf16, scale, tt=512):
    B, T, D = x.shape
    hb = tt // POOL_HALO
    return pl.pallas_call(
        functools.partial(_pool_prompt_kernel, tt=tt),
        grid=(B, T // tt),
        in_specs=[
            pl.BlockSpec((1, tt, D), lambda b, i: (b, i, 0)),
            pl.BlockSpec((1, POOL_HALO, D), lambda b, i: (b, jnp.maximum(i * hb - 1, 0), 0)),
            pl.BlockSpec((1, D), lambda b, i: (0, 0)),
            pl.BlockSpec((len(POOL_WINDOWS), POOL_GROUP, POOL_GROUP), lambda b, i: (0, 0, 0)),
            pl.BlockSpec((1, D), lambda b, i: (0, 0)),
        ],
        out_specs=[
            pl.BlockSpec((1, tt, D), lambda b, i: (b, i, 0)),
            pl.BlockSpec((1, POOL_HALO, D), lambda b, i: (b, 0, 0)),
        ],
        out_shape=[jax.ShapeDtypeStruct((B, T, D), F32), jax.ShapeDtypeStruct((B, POOL_HALO, D), F32)],
        compiler_params=_cparams(("arbitrary", "arbitrary")),
        name="pool_prompt",
    )(x, x, g_mix, w_pool_bf16, scale)


def _pool_sample_kernel(x_ref, hist_ref, g_ref, w_ref, scale_ref, h_ref, u_ref):
    x = x_ref[...]
    u = _rms_scale(x) * g_ref[...]
    u_ref[...] = u
    scale = scale_ref[...]
    for gi, w in enumerate(POOL_WINDOWS):
        sl = slice(gi * POOL_GROUP, (gi + 1) * POOL_GROUP)
        s = u[:, sl]
        for r in range(1, w):
            s = s + hist_ref[POOL_BUF + 1 - r][:, sl]
        d = s / float(w) - u[:, sl]
        z = _dot(d.astype(BF16), w_ref[gi])
        h_ref[:, sl] = x[:, sl] + z * scale[:, sl]


def _pool_sample(x, hist, g_mix, w_pool_bf16, scale):
    DB, D = x.shape
    return pl.pallas_call(
        _pool_sample_kernel,
        out_shape=[jax.ShapeDtypeStruct((DB, D), F32), jax.ShapeDtypeStruct((DB, D), F32)],
        compiler_params=pltpu.CompilerParams(vmem_limit_bytes=VMEM_LIMIT),
        name="pool_sample",
    )(x, hist, g_mix, w_pool_bf16, scale)


def _mlp_kernel(*refs, has_attn, final_norm):
    if has_attn:
        h_ref, o_ref, wo_ref, g_ref, wup_ref, wdn_ref, gfin_ref, out_ref, hres, xn, acc = refs
    else:
        h_ref, g_ref, wup_ref, wdn_ref, gfin_ref, out_ref, hres, xn, acc = refs
    f = pl.program_id(1)

    @pl.when(f == 0)
    def _():
        h = h_ref[...]
        if has_attn:
            h = h + _dot(o_ref[...], wo_ref[...])
        hres[...] = h
        xn[...] = (_rms_scale(h) * g_ref[...]).astype(BF16)
        acc[...] = jnp.zeros_like(acc)

    a = jnp.maximum(_dot(xn[...], wup_ref[...]), 0.0)
    acc[...] += _dot((a * a).astype(BF16), wdn_ref[...])

    @pl.when(f == pl.num_programs(1) - 1)
    def _():
        y = hres[...] + acc[...]
        if final_norm:
            y = _rms_scale(y) * gfin_ref[...]
        out_ref[...] = y


def _mlp(h, g_ffn, w_up, w_down, g_final, o=None, w_o=None, final_norm=False, tm=1024, tf=512):
    M, D = h.shape
    tm = min(tm, M)
    has_attn = o is not None
    row = lambda i, f: (i, 0)
    const = lambda i, f: (0, 0)
    in_specs = [pl.BlockSpec((tm, D), row)]
    args = [h]
    if has_attn:
        in_specs += [pl.BlockSpec((tm, D), row), pl.BlockSpec((D, D), const)]
        args += [o, w_o]
    in_specs += [
        pl.BlockSpec((1, D), const),
        pl.BlockSpec((D, tf), lambda i, f: (0, f)),
        pl.BlockSpec((tf, D), lambda i, f: (f, 0)),
        pl.BlockSpec((1, D), const),
    ]
    args += [g_ffn, w_up, w_down, g_final]
    return pl.pallas_call(
        functools.partial(_mlp_kernel, has_attn=has_attn, final_norm=final_norm),
        grid=(M // tm, D_FF // tf),
        in_specs=in_specs,
        out_specs=pl.BlockSpec((tm, D), row),
        out_shape=jax.ShapeDtypeStruct((M, D), F32),
        scratch_shapes=[pltpu.VMEM((tm, D), F32), pltpu.VMEM((tm, D), BF16), pltpu.VMEM((tm, D), F32)],
        compiler_params=_cparams(("arbitrary", "arbitrary")),
        name="mlp",
    )(*args)


def _proj_prompt_kernel(h_ref, gkv_ref, gq_ref, wkvt_ref, wkp_ref, wqp_ref, wgt_ref, c_ref, sa_ref, sb_ref, ct_ref, st_ref,
                        cmp_ref, sel_ref, win_ref, ksel_ref, kwin_ref, vsel_ref, vwin_ref, q_ref, gate_ref, *, tt):
    i = pl.program_id(1)
    y = _rms_scale(h_ref[0])
    xkv = (y * gkv_ref[...]).astype(BF16)
    xq = (y * gq_ref[...]).astype(BF16)

    kvt = _dot_nt(wkvt_ref[...], xkv)
    cos_t, sin_t = ct_ref[...], st_ref[...]
    for br, (oref, vref) in enumerate(((cmp_ref, None), (sel_ref, vsel_ref), (win_ref, vwin_ref))):
        base = br * ROW_LANES
        for g in range(N_KV):
            r0 = base + g * HEAD_DIM
            x1, x2 = kvt[r0:r0 + ROT_HALF], kvt[r0 + ROT_HALF:r0 + ROT_DIM]
            oref[0, 0, g] = jnp.concatenate(
                [x1 * cos_t - x2 * sin_t, x2 * cos_t + x1 * sin_t, kvt[r0 + ROT_DIM:r0 + HEAD_DIM]], axis=0)
            v = kvt[r0 + KV_LANES:r0 + KV_LANES + HEAD_DIM]
            oref[0, 1, g] = v
            if vref is not None:
                for c in range(tt // LANE):
                    vref[0, g, c] = v[:, c * LANE:(c + 1) * LANE].astype(BF16)

    c, sa, sb = c_ref[...], sa_ref[...], sb_ref[...]
    kp = _rope_lanes(_dot(xkv, wkp_ref[...]), c, sa, sb)
    pos = i * tt + lax.broadcasted_iota(jnp.int32, (tt, 1), 0)
    lane = lax.broadcasted_iota(jnp.int32, (1, HEAD_SLOT), 1)
    blk_onehot = jnp.where(lane - HEAD_DIM == pos // L_SEL, 1.0, 0.0)
    for g in range(N_KV):
        ksel_ref[0, g] = (kp[:, g * HEAD_SLOT:(g + 1) * HEAD_SLOT] + blk_onehot).astype(BF16)
        kwin_ref[0, g] = kp[:, (N_KV + g) * HEAD_SLOT:(N_KV + g + 1) * HEAD_SLOT].astype(BF16)

    q = _rope_lanes(_dot(xq, wqp_ref[...]), c, sa, sb) * (HEAD_DIM ** -0.5)
    q_ref[0] = q.astype(BF16)
    gate_ref[0] = jax.nn.sigmoid(_dot_nt(wgt_ref[...], xq))


def _proj_prompt(h, g_kv, g_q, w_kvt, w_kpad, w_qpad, w_gt, rope_nat, rope_t, tt=512):
    B, T, D = h.shape
    const2 = lambda b, i: (0, 0)
    kv_t = jax.ShapeDtypeStruct((B, 2, N_KV, HEAD_DIM, T), F32)
    k_nat = jax.ShapeDtypeStruct((B, N_KV, T, HEAD_SLOT), BF16)
    v_t = jax.ShapeDtypeStruct((B, N_KV, T // LANE, HEAD_DIM, LANE), BF16)
    kv_spec = pl.BlockSpec((1, 2, N_KV, HEAD_DIM, tt), lambda b, i: (b, 0, 0, 0, i))
    k_spec = pl.BlockSpec((1, N_KV, tt, HEAD_SLOT), lambda b, i: (b, 0, i, 0))
    v_spec = pl.BlockSpec((1, N_KV, tt // LANE, HEAD_DIM, LANE), lambda b, i: (b, 0, i, 0, 0))
    return pl.pallas_call(
        functools.partial(_proj_prompt_kernel, tt=tt),
        grid=(B, T // tt),
        in_specs=[
            pl.BlockSpec((1, tt, D), lambda b, i: (b, i, 0)),
            pl.BlockSpec((1, D), const2), pl.BlockSpec((1, D), const2),
            pl.BlockSpec(w_kvt.shape, const2), pl.BlockSpec(w_kpad.shape, const2),
            pl.BlockSpec(w_qpad.shape, const2), pl.BlockSpec(w_gt.shape, const2),
            pl.BlockSpec((tt, LANE), lambda b, i: (i, 0)), pl.BlockSpec((tt, LANE), lambda b, i: (i, 0)),
            pl.BlockSpec((tt, LANE), lambda b, i: (i, 0)),
            pl.BlockSpec((ROT_HALF, tt), lambda b, i: (0, i)), pl.BlockSpec((ROT_HALF, tt), lambda b, i: (0, i)),
        ],
        out_specs=[kv_spec, kv_spec, kv_spec, k_spec, k_spec, v_spec, v_spec,
                   pl.BlockSpec((1, tt, N_HEADS * HEAD_SLOT), lambda b, i: (b, i, 0)),
                   pl.BlockSpec((1, GATE_LANES, tt), lambda b, i: (b, 0, i))],
        out_shape=[kv_t, kv_t, kv_t, k_nat, k_nat, v_t, v_t,
                   jax.ShapeDtypeStruct((B, T, N_HEADS * HEAD_SLOT), BF16),
                   jax.ShapeDtypeStruct((B, GATE_LANES, T), F32)],
        compiler_params=_cparams(("arbitrary", "arbitrary")),
        name="kv_q_proj_prompt",
    )(h, g_kv, g_q, w_kvt, w_kpad, w_qpad, w_gt, *rope_nat, *rope_t)


def _proj_sample_kernel(h_ref, gkv_ref, gq_ref, wkv_ref, wq_ref, wg_ref, c_ref, sa_ref, sb_ref,
                        cmp_ref, sel_ref, win_ref, q_ref, gate_ref):
    y = _rms_scale(h_ref[...])
    xkv = (y * gkv_ref[...]).astype(BF16)
    xq = (y * gq_ref[...]).astype(BF16)
    c, sa, sb = c_ref[...], sa_ref[...], sb_ref[...]
    kv = _dot(xkv, wkv_ref[...])
    for br, oref in enumerate((cmp_ref, sel_ref, win_ref)):
        oref[:, 0:KV_LANES] = _rope_lanes(kv[:, br * ROW_LANES:br * ROW_LANES + KV_LANES], c, sa, sb)
        oref[:, KV_LANES:ROW_LANES] = kv[:, br * ROW_LANES + KV_LANES:(br + 1) * ROW_LANES]
    q_ref[...] = (_rope_lanes(_dot(xq, wq_ref[...]), c, sa, sb) * (HEAD_DIM ** -0.5)).astype(BF16)
    gate_ref[...] = jax.nn.sigmoid(_dot(xq, wg_ref[...]))


def _proj_sample(h, g_kv, g_q, w_kv, w_q, w_g, rope_nat):
    M, D = h.shape
    return pl.pallas_call(
        _proj_sample_kernel,
        out_shape=[jax.ShapeDtypeStruct((M, ROW_LANES), F32)] * 3
                  + [jax.ShapeDtypeStruct((M, D), BF16), jax.ShapeDtypeStruct((M, GATE_LANES), F32)],
        compiler_params=pltpu.CompilerParams(vmem_limit_bytes=VMEM_LIMIT),
        name="kv_q_proj_sample",
    )(h, g_kv, g_q, w_kv, w_q, w_g, *rope_nat)


def _parts_kernel(*refs, pg, prefetch):
    refs = refs[prefetch:]
    x_refs, w_ref, out_ref, xs = refs[:pg], refs[pg], refs[pg + 1], refs[pg + 2]
    n_rows = pg * SEG_PER_PAGE
    for e in range(2):
        for gp in range(N_KV // 2):
            for p, xr in enumerate(x_refs):
                xt = jnp.concatenate([xr[0, e, 2 * gp], xr[0, e, 2 * gp + 1]], axis=0)
                xs[p * PAGE_SIZE:(p + 1) * PAGE_SIZE, :] = xt.T
            acc = jnp.zeros((n_rows, 2 * 2 * CMP_HIDDEN), F32)
            for rp in range(STRIDE // 2):
                a = xs[pl.ds(2 * rp, n_rows, stride=STRIDE), :]
                b = xs[pl.ds(2 * rp + 1, n_rows, stride=STRIDE), :]
                acc = acc + _dot(jnp.concatenate([a, b], axis=1).astype(BF16), w_ref[e, rp])
            c0 = (e * N_KV + 2 * gp) * 2 * CMP_HIDDEN
            out_ref[0, :, c0:c0 + 2 * 2 * CMP_HIDDEN] = acc


def _cmp_parts(kv_t, w1pair, pages=None, n_b=None, pg=16):
    blk = (1, 2, N_KV, HEAD_DIM, PAGE_SIZE)
    width = 2 * N_KV * 2 * CMP_HIDDEN
    if pages is None:
        n_b, ppb = kv_t.shape[0], kv_t.shape[-1] // PAGE_SIZE
        pg = min(pg, ppb)
        in_specs = [pl.BlockSpec(blk, functools.partial(lambda b, s, p: (b, 0, 0, 0, s * pg + p), p=p)) for p in range(pg)]
        in_specs.append(pl.BlockSpec(w1pair.shape, lambda b, s: (0, 0, 0, 0)))
        out_spec = pl.BlockSpec((1, pg * SEG_PER_PAGE, width), lambda b, s: (b, s, 0))
        grid_kw = dict(grid=(n_b, ppb // pg), in_specs=in_specs, out_specs=out_spec,
                       scratch_shapes=[pltpu.VMEM((pg * PAGE_SIZE, LANE), F32)])
        args = [kv_t] * pg + [w1pair]
        prefetch = 0
    else:
        ppb = pages.shape[0] // n_b
        pg = min(pg, ppb)
        in_specs = [pl.BlockSpec(blk, functools.partial(lambda b, s, pr, p: (pr[b * ppb + s * pg + p], 0, 0, 0, 0), p=p))
                    for p in range(pg)]
        in_specs.append(pl.BlockSpec(w1pair.shape, lambda b, s, pr: (0, 0, 0, 0)))
        out_spec = pl.BlockSpec((1, pg * SEG_PER_PAGE, width), lambda b, s, pr: (b, s, 0))
        grid_kw = dict(grid_spec=pltpu.PrefetchScalarGridSpec(
            num_scalar_prefetch=1, grid=(n_b, ppb // pg), in_specs=in_specs, out_specs=out_spec,
            scratch_shapes=[pltpu.VMEM((pg * PAGE_SIZE, LANE), F32)]))
        args = [pages] + [kv_t] * pg + [w1pair]
        prefetch = 1
    return pl.pallas_call(
        functools.partial(_parts_kernel, pg=pg, prefetch=prefetch),
        out_shape=jax.ShapeDtypeStruct((n_b, ppb * SEG_PER_PAGE, width), F32),
        compiler_params=_cparams(("arbitrary", "arbitrary")),
        name="cmp_parts",
        **grid_kw,
    )(*args)


def _cmp_finish_kernel(parts_ref, pe_ref, w1f_ref, w2p_ref, w2t_ref, knat_ref, kt_ref, vt_ref, *, n_seg):
    rows = lax.broadcasted_iota(jnp.int32, (n_seg, 1), 0)
    for e in range(2):
        pe_sum = _dot(pe_ref[e], w1f_ref[e])[0:1]
        for g in range(N_KV):
            c0 = (e * N_KV + g) * 2 * CMP_HIDDEN
            first = parts_ref[0, :, c0:c0 + CMP_HIDDEN]
            second = pltpu.roll(parts_ref[0, :, c0 + CMP_HIDDEN:c0 + 2 * CMP_HIDDEN], n_seg - 1, axis=0)
            act = jax.nn.gelu(first + second + pe_sum)
            act = jnp.where(rows < n_seg - 1, act, 0.0).astype(BF16)
            o_t = _dot_nt(w2t_ref[e], act).astype(BF16)
            if e == 0:
                knat_ref[0, g] = _dot(act, w2p_ref[...]).astype(BF16)
                kt_ref[0, g] = o_t
            else:
                vt_ref[0, g] = o_t


def _cmp_finish(parts, pe8, w1f, w2pad, w2t):
    n_b, n_seg, width = parts.shape
    c3 = lambda b: (0, 0, 0)
    return pl.pallas_call(
        functools.partial(_cmp_finish_kernel, n_seg=n_seg),
        grid=(n_b,),
        in_specs=[
            pl.BlockSpec((1, n_seg, width), lambda b: (b, 0, 0)),
            pl.BlockSpec(pe8.shape, c3), pl.BlockSpec(w1f.shape, c3),
            pl.BlockSpec(w2pad.shape, lambda b: (0, 0)), pl.BlockSpec(w2t.shape, c3),
        ],
        out_specs=[pl.BlockSpec((1, N_KV, n_seg, HEAD_SLOT), lambda b: (b, 0, 0, 0)),
                   pl.BlockSpec((1, N_KV, HEAD_DIM, n_seg), lambda b: (b, 0, 0, 0)),
                   pl.BlockSpec((1, N_KV, HEAD_DIM, n_seg), lambda b: (b, 0, 0, 0))],
        out_shape=[jax.ShapeDtypeStruct((n_b, N_KV, n_seg, HEAD_SLOT), BF16),
                   jax.ShapeDtypeStruct((n_b, N_KV, HEAD_DIM, n_seg), BF16),
                   jax.ShapeDtypeStruct((n_b, N_KV, HEAD_DIM, n_seg), BF16)],
        compiler_params=_cparams(("arbitrary",)),
        name="cmp_finish",
    )(parts, pe8, w1f, w2pad, w2t)


def _topk_rows(score, n_sel):
    n_blk = score.shape[0]
    n_tiles = n_blk // SUBLANE
    tiles = [score[r * SUBLANE:(r + 1) * SUBLANE] for r in range(n_tiles)]
    rank = [jnp.zeros(tiles[0].shape, F32) for _ in range(n_tiles)]
    sub = lax.broadcasted_iota(jnp.int32, (SUBLANE, 1), 0)
    for i in range(n_blk):
        bi = jnp.broadcast_to(score[i:i + 1], tiles[0].shape)
        ri = i // SUBLANE
        for r in range(n_tiles):
            if r > ri:
                ahead = jnp.where(bi >= tiles[r], 1.0, 0.0)
            elif r < ri:
                ahead = jnp.where(bi > tiles[r], 1.0, 0.0)
            else:
                tie = jnp.where(sub > i % SUBLANE, 1.0, 0.0)
                ahead = jnp.where(bi > tiles[r], 1.0, jnp.where(bi == tiles[r], tie, 0.0))
            rank[r] = rank[r] + ahead
    return jnp.concatenate(rank, axis=0) < float(n_sel)


def _attn_prompt_kernel(q_ref, gt_ref, kc_ref, vc_ref, ks_ref, vs_ref, kw_ref, vw_ref, impt_ref, eye_ref, o_ref,
                        *, tq, kc, t_len):
    i = pl.program_id(1)
    s0 = i * tq
    n_cmp = kc_ref.shape[2]
    n_blk = t_len // L_SEL
    rows = HPG * tq
    qpos = s0 + lax.broadcasted_iota(jnp.int32, (1, tq), 1)
    qpos4 = jnp.concatenate([qpos] * HPG, axis=1)
    blk = lax.broadcasted_iota(jnp.int32, (n_blk, 1), 0)
    cur = qpos // L_SEL
    valid = blk <= cur
    forced = (blk == 0) | (blk == cur) | (blk == cur - 1)
    c_end = lax.broadcasted_iota(jnp.int32, (n_cmp, 1), 0) * STRIDE + (L_CMP - 1)
    cmask = c_end <= qpos4
    slot_lane = lax.broadcasted_iota(jnp.int32, (1, HEAD_SLOT), 1)
    n_full = s0 // kc
    w_len = WINDOW + tq
    w0 = pl.multiple_of(jnp.clip(s0 - WINDOW, 0, t_len - w_len), LANE)
    wpos = w0 + lax.broadcasted_iota(jnp.int32, (w_len, 1), 0)
    wmask = (wpos <= qpos4) & (wpos >= qpos4 - WINDOW)
    gt = gt_ref[0]
    heads = []

    def flash_update(s, v, carry):
        m_i, l_i, acc = carry
        m_new = jnp.maximum(m_i, jnp.max(s, axis=0, keepdims=True))
        alpha = jnp.exp(m_i - m_new)
        e = jnp.exp(s - m_new)
        return m_new, alpha * l_i + jnp.sum(e, axis=0, keepdims=True), alpha * acc + _dot(v, e.astype(BF16))

    for g in range(N_KV):
        qg = jnp.concatenate([q_ref[0, :, (g * HPG + j) * HEAD_SLOT:(g * HPG + j + 1) * HEAD_SLOT] for j in range(HPG)], axis=0)

        p_c = _masked_softmax(_dot_nt(kc_ref[0, g], qg), cmask, 0)
        o_c = _dot(vc_ref[0, g], p_c.astype(BF16))
        p_sum = p_c[:, 0:tq]
        for j in range(1, HPG):
            p_sum = p_sum + p_c[:, j * tq:(j + 1) * tq]
        imp = _dot_exact_rhs(impt_ref[...], p_sum)
        score = jnp.where(forced, FORCE_SCORE, jnp.where(valid, imp, -1.0))
        sel = _topk_rows(score, min(N_SEL, n_blk)) & valid

        pieces = [jnp.ones((HEAD_DIM, tq), F32), jnp.where(sel, 1.0, 0.0)]
        if n_blk < HEAD_SLOT - HEAD_DIM:
            pieces.append(jnp.zeros((HEAD_SLOT - HEAD_DIM - n_blk, tq), F32))
        sel_pad = jnp.concatenate(pieces, axis=0)
        sel_q = _dot_nt(eye_ref[...], sel_pad.astype(BF16))
        bias = ((sel_q - 1.0) * -NEG_INF).astype(BF16)
        qa = jnp.where(slot_lane < HEAD_DIM, qg, jnp.concatenate([bias] * HPG, axis=0))

        def scores(c):
            return _dot_nt(ks_ref[0, g, pl.ds(pl.multiple_of(c * kc, kc), kc), :], qa)

        def values(c):
            return jnp.concatenate([vs_ref[0, g, c * (kc // LANE) + t] for t in range(kc // LANE)], axis=1)

        def chunk(c, carry):
            s_next = scores(c + 1)
            return flash_update(carry[3], values(c), carry[:3]) + (s_next,)

        init = (jnp.full((1, rows), NEG_INF, F32), jnp.zeros((1, rows), F32), jnp.zeros((HEAD_DIM, rows), F32), scores(0))
        carry = lax.fori_loop(0, n_full, chunk, init)
        kpos = n_full * kc + lax.broadcasted_iota(jnp.int32, (kc, 1), 0)
        _, l_s, acc_s = flash_update(jnp.where(kpos <= qpos4, carry[3], NEG_INF), values(n_full), carry[:3])
        o_s = acc_s / l_s

        p_w = _masked_softmax(_dot_nt(kw_ref[0, g, pl.ds(w0, w_len), :], qg), wmask, 0)
        v_w = jnp.concatenate([vw_ref[0, g, w0 // LANE + t] for t in range(w_len // LANE)], axis=1)
        o_w = _dot(v_w, p_w.astype(BF16))

        def gate_row(br):
            return jnp.concatenate([gt[(g * HPG + j) * N_BRANCH + br:(g * HPG + j) * N_BRANCH + br + 1] for j in range(HPG)], axis=1)

        comb = gate_row(0) * o_c + gate_row(1) * o_s + gate_row(2) * o_w
        heads += [comb[:, j * tq:(j + 1) * tq] for j in range(HPG)]

    o_ref[0] = jnp.concatenate(heads, axis=0).T.astype(BF16)


def _attn_prompt(q_pad, gate_t, kc_nat, vc_t, ksel, vsel, kwin, vwin, imp_t, tq=128, kc=512):
    B, T, _ = q_pad.shape
    n_cmp = kc_nat.shape[2]
    eye = jnp.eye(tq, dtype=BF16)
    b4 = lambda b, i: (b, 0, 0, 0)
    b5 = lambda b, i: (b, 0, 0, 0, 0)
    return pl.pallas_call(
        functools.partial(_attn_prompt_kernel, tq=tq, kc=kc, t_len=T),
        grid=(B, T // tq),
        in_specs=[
            pl.BlockSpec((1, tq, N_HEADS * HEAD_SLOT), lambda b, i: (b, i, 0)),
            pl.BlockSpec((1, GATE_LANES, tq), lambda b, i: (b, 0, i)),
            pl.BlockSpec((1, N_KV, n_cmp, HEAD_SLOT), b4),
            pl.BlockSpec((1, N_KV, HEAD_DIM, n_cmp), b4),
            pl.BlockSpec((1, N_KV, T, HEAD_SLOT), b4),
            pl.BlockSpec((1, N_KV, T // LANE, HEAD_DIM, LANE), b5),
            pl.BlockSpec((1, N_KV, T, HEAD_SLOT), b4),
            pl.BlockSpec((1, N_KV, T // LANE, HEAD_DIM, LANE), b5),
            pl.BlockSpec(imp_t.shape, lambda b, i: (0, 0)),
            pl.BlockSpec(eye.shape, lambda b, i: (0, 0)),
        ],
        out_specs=pl.BlockSpec((1, tq, D_MODEL), lambda b, i: (b, i, 0)),
        out_shape=jax.ShapeDtypeStruct((B, T, D_MODEL), BF16),
        compiler_params=_cparams(("arbitrary", "arbitrary")),
        name="attn_prompt",
    )(q_pad, gate_t, kc_nat, vc_t, ksel, vsel, kwin, vwin, imp_t, eye)


def _sample_cmp_kernel(q_ref, kt_ref, vt_ref, imp_ref, oc_ref, idx_ref, val_ref, *, n_valid, cur, n_blk_lanes):
    n_cmp = kt_ref.shape[3]
    cmask = lax.broadcasted_iota(jnp.int32, (1, n_cmp), 1) < n_valid
    sums = []
    for g in range(N_KV):
        p = _masked_softmax(_dot(q_ref[0, g].astype(BF16), kt_ref[0, g]), cmask, -1)
        oc_ref[0, g] = _dot_nt(p.astype(BF16), vt_ref[0, g])
        sums.append(jnp.sum(p[0:HPG], axis=0, keepdims=True))
    imp = _dot_exact_lhs(jnp.concatenate(sums, axis=0), imp_ref[...])
    blk = lax.broadcasted_iota(jnp.int32, (1, n_blk_lanes), 1)
    blk_f = blk.astype(F32)
    valid = blk <= cur
    forced = (blk == 0) | (blk == cur) | (blk == cur - 1)
    score = jnp.where(forced, FORCE_SCORE, jnp.where(valid, imp, -1.0))
    score = jnp.where(valid, score, -2.0)
    out_lane = lax.broadcasted_iota(jnp.int32, (1, LANE), 1)
    idx_acc = jnp.zeros((N_KV, LANE), F32)
    val_acc = jnp.full((N_KV, LANE), -1.0, F32)
    for t in range(N_SEL):
        mx = jnp.max(score, axis=-1, keepdims=True)
        ix = jnp.min(jnp.where(score == mx, blk_f, float(n_blk_lanes)), axis=-1, keepdims=True)
        idx_acc = jnp.where(out_lane == t, ix, idx_acc)
        val_acc = jnp.where(out_lane == t, mx, val_acc)
        score = jnp.where(blk_f == ix, -3.0, score)
    idx_ref[0] = idx_acc.astype(jnp.int32)
    val_ref[0] = val_acc


def _sample_cmp(q4, kc_t, vc_t, imp_mat, n_valid, cur):
    DB = q4.shape[0]
    n_cmp = kc_t.shape[3]
    b4 = lambda b: (b, 0, 0, 0)
    return pl.pallas_call(
        functools.partial(_sample_cmp_kernel, n_valid=n_valid, cur=cur, n_blk_lanes=imp_mat.shape[1]),
        grid=(DB,),
        in_specs=[
            pl.BlockSpec((1, N_KV, JPAD, HEAD_DIM), b4),
            pl.BlockSpec((1, N_KV, HEAD_DIM, n_cmp), b4),
            pl.BlockSpec((1, N_KV, HEAD_DIM, n_cmp), b4),
            pl.BlockSpec(imp_mat.shape, lambda b: (0, 0)),
        ],
        out_specs=[
            pl.BlockSpec((1, N_KV, JPAD, HEAD_DIM), b4),
            pl.BlockSpec((1, N_KV, LANE), lambda b: (b, 0, 0)),
            pl.BlockSpec((1, N_KV, LANE), lambda b: (b, 0, 0)),
        ],
        out_shape=[jax.ShapeDtypeStruct((DB, N_KV, JPAD, HEAD_DIM), F32),
                   jax.ShapeDtypeStruct((DB, N_KV, LANE), jnp.int32),
                   jax.ShapeDtypeStruct((DB, N_KV, LANE), F32)],
        compiler_params=_cparams(("arbitrary",)),
        name="sample_cmp_topk",
    )(q4, kc_t, vc_t, imp_mat)


def _sample_sel_kernel(pg_ref, idx_ref, ok_ref, b0_ref, b1_ref, b2_ref, b3_ref, q_ref, gate_ref, selnew_ref,
                       win_ref, winnew_ref, oc_ref, o_ref, m_sc, l_sc, acc_sc, *, qpos, nb_past):
    del pg_ref
    b = pl.program_id(0)
    k = pl.program_id(1)
    bpp = PAGE_SIZE // L_SEL

    @pl.when(k == 0)
    def _():
        m_sc[...] = jnp.full(m_sc.shape, NEG_INF, F32)
        l_sc[...] = jnp.zeros_like(l_sc)
        acc_sc[...] = jnp.zeros_like(acc_sc)

    lane = lax.broadcasted_iota(jnp.int32, (1, PAGE_SIZE), 1)
    for g, bref in enumerate((b0_ref, b1_ref, b2_ref, b3_ref)):
        n = (b * N_KV + g) * N_SEL + k
        idx = idx_ref[n]
        tail = idx >= nb_past
        new_col = (lane + jnp.where(tail, 0, PAGE_SIZE)) == 0
        k_t = jnp.where(new_col, selnew_ref[0, 0, g], bref[0, 0, 0])
        v_t = jnp.where(new_col, selnew_ref[0, 1, g], bref[0, 1, 0])
        s = _dot(q_ref[0, g].astype(BF16), k_t.astype(BF16))
        first_pos = jnp.where(tail, idx * L_SEL, (idx // bpp) * PAGE_SIZE)
        half = jnp.where(tail, 0, idx % bpp)
        kpos = first_pos + lane + jnp.where(ok_ref[n] > 0, 0, qpos + 1)
        keep = (lane // L_SEL == half) & (kpos <= qpos)
        sm = jnp.where(keep, s, NEG_INF)
        m_new = jnp.maximum(m_sc[g], jnp.max(sm, axis=-1, keepdims=True))
        alpha = jnp.exp(m_sc[g] - m_new)
        e = jnp.where(keep, jnp.exp(sm - m_new), 0.0)
        l_sc[g] = alpha * l_sc[g] + jnp.sum(e, axis=-1, keepdims=True)
        acc_sc[g] = alpha * acc_sc[g] + _dot_nt(e.astype(BF16), v_t.astype(BF16))
        m_sc[g] = m_new

    @pl.when(k == pl.num_programs(1) - 1)
    def _():
        n_win = win_ref.shape[4]
        ext_lane = lax.broadcasted_iota(jnp.int32, (1, LANE), 1)
        wpos = qpos - n_win + lax.broadcasted_iota(jnp.int32, (1, n_win + LANE), 1)
        wmask = (wpos >= 0) & (wpos <= qpos) & (wpos >= qpos - WINDOW)
        for g in range(N_KV):
            k_t = jnp.concatenate([win_ref[0, 0, g], jnp.where(ext_lane == 0, winnew_ref[0, 0, g], 0.0)], axis=1)
            v_t = jnp.concatenate([win_ref[0, 1, g], jnp.where(ext_lane == 0, winnew_ref[0, 1, g], 0.0)], axis=1)
            p_w = _masked_softmax(_dot(q_ref[0, g].astype(BF16), k_t.astype(BF16)), wmask, -1)
            o_w = _dot_nt(p_w.astype(BF16), v_t.astype(BF16))
            gate = gate_ref[0, g]
            o_ref[0, g] = gate[:, 0:1] * oc_ref[0, g] + gate[:, 1:2] * (acc_sc[g] / l_sc[g]) + gate[:, 2:3] * o_w


def _sample_sel(pages, idx, ok, cache_t, q4, gate4, sel_new_t, win_t, win_new_t, o_c, qpos, nb_past):
    DB = q4.shape[0]
    n_win = win_t.shape[4]

    def page_spec(g):
        return pl.BlockSpec((1, 2, 1, HEAD_DIM, PAGE_SIZE),
                            lambda b, k, pg_ref, idx_ref, ok_ref: (pg_ref[(b * N_KV + g) * N_SEL + k], 0, g, 0, 0))

    b4 = lambda b, k, *_: (b, 0, 0, 0)
    b5 = lambda b, k, *_: (b, 0, 0, 0, 0)
    grid_spec = pltpu.PrefetchScalarGridSpec(
        num_scalar_prefetch=3,
        grid=(DB, N_SEL),
        in_specs=[page_spec(g) for g in range(N_KV)] + [
            pl.BlockSpec((1, N_KV, JPAD, HEAD_DIM), b4),
            pl.BlockSpec((1, N_KV, JPAD, N_BRANCH), b4),
            pl.BlockSpec((1, 2, N_KV, HEAD_DIM, 1), b5),
            pl.BlockSpec((1, 2, N_KV, HEAD_DIM, n_win), b5),
            pl.BlockSpec((1, 2, N_KV, HEAD_DIM, 1), b5),
            pl.BlockSpec((1, N_KV, JPAD, HEAD_DIM), b4),
        ],
        out_specs=pl.BlockSpec((1, N_KV, JPAD, HEAD_DIM), b4),
        scratch_shapes=[pltpu.VMEM((N_KV, JPAD, 1), F32), pltpu.VMEM((N_KV, JPAD, 1), F32),
                        pltpu.VMEM((N_KV, JPAD, HEAD_DIM), F32)],
    )
    return pl.pallas_call(
        functools.partial(_sample_sel_kernel, qpos=qpos, nb_past=nb_past),
        grid_spec=grid_spec,
        out_shape=jax.ShapeDtypeStruct((DB, N_KV, JPAD, HEAD_DIM), F32),
        compiler_params=_cparams(("arbitrary", "arbitrary")),
        name="sample_sel_win",
    )(pages, idx, ok, *([cache_t] * N_KV), q4, gate4, sel_new_t, win_t, win_new_t, o_c)


def _imp_matrix(n_cmp_rows, n_blk_cols):
    m = np.zeros((n_cmp_rows, n_blk_cols), np.float32)
    for j in range(n_blk_cols):
        for a in range(SEL_RATIO):
            for c in range(CMP_PARTS):
                i = SEL_RATIO * j + a - c
                if 0 <= i < n_cmp_rows:
                    m[i, j] += 1.0
    return m


def _rope_angles(pos):
    inv = ROPE_THETA ** (-jnp.arange(0, ROT_DIM, 2, dtype=F32) / ROT_DIM)
    ang = pos.astype(F32)[:, None] * inv[None, :]
    return jnp.cos(ang), jnp.sin(ang)


def _rope_tables(pos, period):
    cos, sin = _rope_angles(pos)
    n = pos.shape[0]
    rest1, rest0, zh = jnp.ones((n, period - ROT_DIM), F32), jnp.zeros((n, period - ROT_DIM), F32), jnp.zeros_like(sin)
    c = jnp.concatenate([cos, cos, rest1], axis=1)
    sa = jnp.concatenate([-sin, zh, rest0], axis=1)
    sb = jnp.concatenate([zh, sin, rest0], axis=1)
    return tuple(jnp.tile(t, (1, LANE // period)) for t in (c, sa, sb))


def _pad_heads(w, n_heads):
    w = w.reshape(w.shape[0], n_heads, HEAD_DIM)
    return jnp.pad(w, ((0, 0), (0, 0), (0, HEAD_SLOT - HEAD_DIM))).reshape(w.shape[0], n_heads * HEAD_SLOT)


def _to_rows(kv_t):
    return jnp.transpose(kv_t, (0, 4, 1, 2, 3))


def _to_cols(kv):
    return jnp.transpose(kv, (0, 2, 3, 4, 1))


def kernel(x_prompt, x_sample, state_pool, cache_cmp_kv, cache_sel_kv, state_win_kv, page_table,
           norm_mix, norm_ffn, pool_w, pool_scale, w_qg, w_o, norm_kv, w_kv, cmp_pe, cmp_w1, cmp_w2,
           mlp_up, mlp_down, norm_final):
    B, T, D = x_prompt.shape
    DB, S, _ = x_sample.shape
    past_len = page_table.shape[1] * PAGE_SIZE
    n_blk_p = T // L_SEL
    assert S == 1 and D == D_MODEL and T % PAGE_SIZE == 0 and n_blk_p <= HEAD_SLOT - HEAD_DIM and n_blk_p % SUBLANE == 0

    n_q = N_HEADS * HEAD_DIM
    w_q = w_qg[0][:, :n_q]
    w_gate = jnp.pad(w_qg[0][:, n_q:], ((0, 0), (0, GATE_LANES - N_HEADS * N_BRANCH)))
    w_kv5 = w_kv.reshape(D, N_BRANCH, 2, N_KV * HEAD_DIM)
    w_kpad = jnp.concatenate([_pad_heads(w_kv5[:, 1, 0], N_KV), _pad_heads(w_kv5[:, 2, 0], N_KV)], axis=1).astype(BF16)
    w_kv_b = w_kv.astype(BF16)
    w_o_b = w_o[0].astype(BF16)
    pool_w_b = pool_w[0].astype(BF16)
    up_b, down_b = mlp_up.astype(BF16), mlp_down.astype(BF16)
    g_mix0, g_mix1 = norm_mix[0:1], norm_mix[1:2]
    g_kv, g_fin = norm_kv[None, :], norm_final[None, :]
    w1_6 = cmp_w1.reshape(2, CMP_PARTS, STRIDE // 2, 2, HEAD_DIM, CMP_HIDDEN)
    w1pair = jnp.einsum('ab,eprshk->ersahbpk', jnp.eye(2, dtype=F32), w1_6)
    w1pair = w1pair.reshape(2, STRIDE // 2, 2 * 2 * HEAD_DIM, 2 * CMP_PARTS * CMP_HIDDEN).astype(BF16)
    w1_flat = cmp_w1.reshape(2, L_CMP * HEAD_DIM, CMP_HIDDEN).astype(BF16)
    pe8 = jnp.broadcast_to(cmp_pe.transpose(1, 0, 2).reshape(2, 1, L_CMP * HEAD_DIM), (2, SUBLANE, L_CMP * HEAD_DIM)).astype(BF16)
    w2pad = jnp.pad(cmp_w2[0], ((0, 0), (0, HEAD_SLOT - HEAD_DIM))).astype(BF16)
    w2t = cmp_w2.transpose(0, 2, 1).astype(BF16)

    h1, pool_tail = _pool_prompt(x_prompt, g_mix0, pool_w_b, pool_scale)
    pool_prompt = pool_tail[None, :, POOL_HALO - POOL_BUF:]
    h2 = _mlp(h1.reshape(B * T, D), norm_ffn[0:1], up_b[0], down_b[0], g_fin)
    pos_p = jnp.arange(T)
    cos_p, sin_p = _rope_angles(pos_p)
    cmp_t, sel_t, win_t, ksel, kwin, vsel, vwin, q_pad, gate_t = _proj_prompt(
        h2.reshape(B, T, D), g_kv, g_mix1, w_kv_b.T, w_kpad, _pad_heads(w_q, N_HEADS).astype(BF16), w_gate.T.astype(BF16),
        _rope_tables(pos_p, HEAD_SLOT), (cos_p.T, sin_p.T))
    kc_nat_p, _, vc_t_p = _cmp_finish(_cmp_parts(cmp_t, w1pair), pe8, w1_flat, w2pad, w2t)
    imp_t = jnp.asarray(_imp_matrix(kc_nat_p.shape[2], n_blk_p).T, BF16)
    o_p = _attn_prompt(q_pad, gate_t, kc_nat_p, vc_t_p, ksel, vsel, kwin, vwin, imp_t)
    y_p = _mlp(h2, norm_ffn[1:2], up_b[1], down_b[1], g_fin, o=o_p.reshape(B * T, D), w_o=w_o_b, final_norm=True)

    hist = jnp.concatenate([jnp.zeros((1, DB, D), F32), state_pool[0].transpose(1, 0, 2)], axis=0)
    hs1, us = _pool_sample(x_sample[:, 0], hist, g_mix0, pool_w_b, pool_scale)
    pool_sample = jnp.concatenate([state_pool[0][:, 1:], us[:, None]], axis=1)[None]
    hs2 = _mlp(hs1, norm_ffn[0:1], up_b[0], down_b[0], g_fin)
    cmp_s, sel_s, win_s, q_s, gate_s = _proj_sample(
        hs2, g_kv, g_mix1, w_kv_b, w_q.astype(BF16), w_gate.astype(BF16),
        _rope_tables(jnp.full((DB,), past_len, jnp.int32), HEAD_DIM))
    parts_s = _cmp_parts(_to_cols(cache_cmp_kv), w1pair, pages=page_table.reshape(-1), n_b=DB)
    _, kc_t_s, vc_t_s = _cmp_finish(parts_s, pe8, w1_flat, w2pad, w2t)
    n_valid = past_len // STRIDE - CMP_PARTS + 1
    nb_past = past_len // L_SEL
    n_blk_lanes = -(-(nb_past + 1) // LANE) * LANE
    jpad = ((0, 0), (0, 0), (0, JPAD - HPG), (0, 0))
    q4 = jnp.pad(q_s.astype(F32).reshape(DB, N_KV, HPG, HEAD_DIM), jpad)
    gate4 = jnp.pad(gate_s[:, :N_HEADS * N_BRANCH].reshape(DB, N_KV, HPG, N_BRANCH), jpad)
    imp_s = jnp.asarray(_imp_matrix(kc_t_s.shape[3], n_blk_lanes), BF16)
    o_c, idx_pad, val_pad = _sample_cmp(q4, kc_t_s, vc_t_s, imp_s, n_valid, nb_past)
    idx = idx_pad[:, :, :N_SEL]
    ok = (val_pad[:, :, :N_SEL] >= 0).astype(jnp.int32)
    bpp = PAGE_SIZE // L_SEL
    logical_page = (jnp.minimum(idx, nb_past - 1) // bpp).reshape(DB, -1)
    pages = jnp.take_along_axis(page_table, logical_page, axis=1)
    kv5 = lambda a, n, t: a.reshape(n, t, 2, N_KV, HEAD_DIM)
    o_s = _sample_sel(pages.reshape(-1), idx.reshape(-1), ok.reshape(-1), _to_cols(cache_sel_kv), q4, gate4,
                      _to_cols(kv5(sel_s, DB, 1)), _to_cols(state_win_kv), _to_cols(kv5(win_s, DB, 1)), o_c,
                      past_len, nb_past)
    o_s = o_s[:, :, :HPG].reshape(DB, D).astype(BF16)
    y_s = _mlp(hs2, norm_ffn[1:2], up_b[1], down_b[1], g_fin, o=o_s, w_o=w_o_b, final_norm=True)

    win_sample = jnp.concatenate([state_win_kv, kv5(win_s, DB, 1)], axis=1)[:, S:]
    return (y_p.reshape(B, T, D), y_s.reshape(DB, S, D),
            _to_rows(cmp_t), _to_rows(sel_t), _to_rows(win_t[..., T - min(WINDOW, T):]), pool_prompt,
            kv5(cmp_s, DB, 1), kv5(sel_s, DB, 1), win_sample, pool_sample)
```

```python
import functools
import math

import numpy as np
import jax
import jax.numpy as jnp
from jax import lax
from jax.experimental import pallas as pl
from jax.experimental.pallas import tpu as pltpu

D_MODEL = 1024
POOL_WINDOWS = (2, 4, 8, 16)
POOL_GROUP = D_MODEL // len(POOL_WINDOWS)
POOL_BUF = max(POOL_WINDOWS) - 1
POOL_HALO = 16
N_HEADS = 16
N_KV = 4
HPG = N_HEADS // N_KV
HEAD_DIM = 64
KV_LANES = N_KV * HEAD_DIM
ROW_LANES = 2 * KV_LANES
ROT_DIM = HEAD_DIM // 4
ROT_HALF = ROT_DIM // 2
ROPE_THETA = 500000.0
L_CMP = 32
STRIDE = 16
CMP_HIDDEN = 2 * HEAD_DIM
L_SEL = 64
SEL_RATIO = L_SEL // STRIDE
CMP_PARTS = L_CMP // STRIDE
N_SEL = 16
WINDOW = 512
N_BRANCH = 3
D_FF = 4 * D_MODEL
RMS_EPS = 1e-6
NEG_INF = -1e30
FORCE_SCORE = 1e9
Q_SCALE_LOG2 = HEAD_DIM ** -0.5 * math.log2(math.e)
PAGE_SIZE = 128
SEG_PER_PAGE = PAGE_SIZE // STRIDE
GATE_LANES = 128
LANE = 128
SUBLANE = 8
HEAD_SLOT = LANE
JPAD = SUBLANE

VMEM_LIMIT = 56 * 1024 * 1024

F32 = jnp.float32
BF16 = jnp.bfloat16


def _cparams(sem):
    return pltpu.CompilerParams(dimension_semantics=sem, vmem_limit_bytes=VMEM_LIMIT)


def _dot(a, b):
    return jnp.dot(a, b, preferred_element_type=F32)


def _dot_nt(a, b):
    return lax.dot_general(a, b, (((1,), (1,)), ((), ())), preferred_element_type=F32)


def _split3(x):
    hi = x.astype(BF16)
    r1 = x - hi.astype(F32)
    mid = r1.astype(BF16)
    lo = (r1 - mid.astype(F32)).astype(BF16)
    return hi, mid, lo


def _dot_exact_lhs(x, m_bf16):
    hi, mid, lo = _split3(x)
    return _dot(hi, m_bf16) + _dot(mid, m_bf16) + _dot(lo, m_bf16)


def _dot_exact_rhs(m_bf16, x):
    hi, mid, lo = _split3(x)
    return _dot(m_bf16, hi) + _dot(m_bf16, mid) + _dot(m_bf16, lo)


def _rms_scale(x):
    return x * lax.rsqrt(jnp.mean(x * x, axis=-1, keepdims=True) + RMS_EPS)


def _rope_lanes(x, c, sa, sb):
    outs = []
    for t in range(x.shape[1] // LANE):
        xc = x[:, t * LANE:(t + 1) * LANE]
        outs.append(xc * c + pltpu.roll(xc, LANE - ROT_HALF, axis=1) * sa + pltpu.roll(xc, ROT_HALF, axis=1) * sb)
    return jnp.concatenate(outs, axis=1)


def _masked_softmax(s, mask, axis, base2=False):
    sm = jnp.where(mask, s, NEG_INF)
    m = jnp.max(sm, axis=axis, keepdims=True)
    e = jnp.where(mask, (jnp.exp2 if base2 else jnp.exp)(sm - m), 0.0)
    l = jnp.sum(e, axis=axis, keepdims=True)
    return e * (1.0 / jnp.where(l > 0.0, l, 1.0))


def _pool_prompt_kernel(x_ref, xprev_ref, g_ref, w_ref, scale_ref, h_ref, buf_ref, *, tt):
    i = pl.program_id(1)
    x = x_ref[0]
    g = g_ref[...]
    u = _rms_scale(x) * g
    up = _rms_scale(xprev_ref[0]) * g
    up = jnp.where(i > 0, up, 0.0)
    ext = jnp.concatenate([up, u], axis=0)
    pos = i * tt + lax.broadcasted_iota(jnp.int32, (tt, 1), 0)
    scale = scale_ref[...]
    for gi, w in enumerate(POOL_WINDOWS):
        sl = slice(gi * POOL_GROUP, (gi + 1) * POOL_GROUP)
        s = ext[:, sl]
        k = 1
        while k < w:
            s = s + pltpu.roll(s, k, axis=0)
            k *= 2
        s = s[POOL_HALO:]
        cnt = jnp.minimum(pos + 1, w).astype(F32)
        d = s / cnt - u[:, sl]
        z = _dot(d.astype(BF16), w_ref[gi])
        h_ref[0, :, sl] = x[:, sl] + z * scale[:, sl]
    buf_ref[0] = u[tt - POOL_HALO:]


def _pool_prompt(x, g_mix, w_pool_bf16, scale, tt=512):
    B, T, D = x.shape
    hb = tt // POOL_HALO
    return pl.pallas_call(
        functools.partial(_pool_prompt_kernel, tt=tt),
        grid=(B, T // tt),
        in_specs=[
            pl.BlockSpec((1, tt, D), lambda b, i: (b, i, 0)),
            pl.BlockSpec((1, POOL_HALO, D), lambda b, i: (b, jnp.maximum(i * hb - 1, 0), 0)),
            pl.BlockSpec((1, D), lambda b, i: (0, 0)),
            pl.BlockSpec((len(POOL_WINDOWS), POOL_GROUP, POOL_GROUP), lambda b, i: (0, 0, 0)),
            pl.BlockSpec((1, D), lambda b, i: (0, 0)),
        ],
        out_specs=[
            pl.BlockSpec((1, tt, D), lambda b, i: (b, i, 0)),
            pl.BlockSpec((1, POOL_HALO, D), lambda b, i: (b, 0, 0)),
        ],
        out_shape=[jax.ShapeDtypeStruct((B, T, D), F32), jax.ShapeDtypeStruct((B, POOL_HALO, D), F32)],
        compiler_params=_cparams(("arbitrary", "arbitrary")),
        name="pool_prompt",
    )(x, x, g_mix, w_pool_bf16, scale)


def _pool_sample_kernel(x_ref, hist_ref, g_ref, w_ref, scale_ref, h_ref, u_ref):
    x = x_ref[...]
    u = _rms_scale(x) * g_ref[...]
    u_ref[...] = u
    scale = scale_ref[...]
    for gi, w in enumerate(POOL_WINDOWS):
        sl = slice(gi * POOL_GROUP, (gi + 1) * POOL_GROUP)
        s = u[:, sl]
        for r in range(1, w):
            s = s + hist_ref[POOL_BUF + 1 - r][:, sl]
        d = s / float(w) - u[:, sl]
        z = _dot(d.astype(BF16), w_ref[gi])
        h_ref[:, sl] = x[:, sl] + z * scale[:, sl]


def _pool_sample(x, hist, g_mix, w_pool_bf16, scale):
    DB, D = x.shape
    return pl.pallas_call(
        _pool_sample_kernel,
        out_shape=[jax.ShapeDtypeStruct((DB, D), F32), jax.ShapeDtypeStruct((DB, D), F32)],
        compiler_params=pltpu.CompilerParams(vmem_limit_bytes=VMEM_LIMIT),
        name="pool_sample",
    )(x, hist, g_mix, w_pool_bf16, scale)


def _mlp_kernel(*refs, has_attn, final_norm):
    if has_attn:
        h_ref, o_ref, wo_ref, g_ref, wup_ref, wdn_ref, gfin_ref, out_ref, hres, xn, acc = refs
    else:
        h_ref, g_ref, wup_ref, wdn_ref, gfin_ref, out_ref, hres, xn, acc = refs
    f = pl.program_id(1)

    @pl.when(f == 0)
    def _():
        h = h_ref[...]
        if has_attn:
            h = h + _dot(o_ref[...], wo_ref[...])
        hres[...] = h
        xn[...] = (_rms_scale(h) * g_ref[...]).astype(BF16)
        acc[...] = jnp.zeros_like(acc)

    a = jnp.maximum(_dot(xn[...], wup_ref[...]), 0.0)
    acc[...] += _dot((a * a).astype(BF16), wdn_ref[...])

    @pl.when(f == pl.num_programs(1) - 1)
    def _():
        y = hres[...] + acc[...]
        if final_norm:
            y = _rms_scale(y) * gfin_ref[...]
        out_ref[...] = y


def _mlp(h, g_ffn, w_up, w_down, g_final, o=None, w_o=None, final_norm=False, tm=1024, tf=512):
    M, D = h.shape
    tm = min(tm, M)
    has_attn = o is not None
    row = lambda i, f: (i, 0)
    const = lambda i, f: (0, 0)
    in_specs = [pl.BlockSpec((tm, D), row)]
    args = [h]
    if has_attn:
        in_specs += [pl.BlockSpec((tm, D), row), pl.BlockSpec((D, D), const)]
        args += [o, w_o]
    in_specs += [
        pl.BlockSpec((1, D), const),
        pl.BlockSpec((D, tf), lambda i, f: (0, f)),
        pl.BlockSpec((tf, D), lambda i, f: (f, 0)),
        pl.BlockSpec((1, D), const),
    ]
    args += [g_ffn, w_up, w_down, g_final]
    return pl.pallas_call(
        functools.partial(_mlp_kernel, has_attn=has_attn, final_norm=final_norm),
        grid=(M // tm, D_FF // tf),
        in_specs=in_specs,
        out_specs=pl.BlockSpec((tm, D), row),
        out_shape=jax.ShapeDtypeStruct((M, D), F32),
        scratch_shapes=[pltpu.VMEM((tm, D), F32), pltpu.VMEM((tm, D), BF16), pltpu.VMEM((tm, D), F32)],
        compiler_params=_cparams(("arbitrary", "arbitrary")),
        name="mlp",
    )(*args)


def _proj_prompt_kernel(h_ref, gkv_ref, gq_ref, wkvt_ref, wkp_ref, wqp_ref, wgt_ref, c_ref, sa_ref, sb_ref, ct_ref, st_ref,
                        cmp_ref, sel_ref, win_ref, ksel_ref, kwin_ref, vsel_ref, vwin_ref, q_ref, gate_ref, *, tt):
    i = pl.program_id(1)
    y = _rms_scale(h_ref[0])
    xkv = (y * gkv_ref[...]).astype(BF16)
    xq = (y * gq_ref[...]).astype(BF16)

    kvt = _dot_nt(wkvt_ref[...], xkv)
    cos_t, sin_t = ct_ref[...], st_ref[...]
    for br, (oref, vref) in enumerate(((cmp_ref, None), (sel_ref, vsel_ref), (win_ref, vwin_ref))):
        base = br * ROW_LANES
        for g in range(N_KV):
            r0 = base + g * HEAD_DIM
            x1, x2 = kvt[r0:r0 + ROT_HALF], kvt[r0 + ROT_HALF:r0 + ROT_DIM]
            oref[0, 0, g] = jnp.concatenate(
                [x1 * cos_t - x2 * sin_t, x2 * cos_t + x1 * sin_t, kvt[r0 + ROT_DIM:r0 + HEAD_DIM]], axis=0)
            v = kvt[r0 + KV_LANES:r0 + KV_LANES + HEAD_DIM]
            oref[0, 1, g] = v
            if vref is not None:
                for c in range(tt // LANE):
                    vref[0, g, c] = v[:, c * LANE:(c + 1) * LANE].astype(BF16)

    c, sa, sb = c_ref[...], sa_ref[...], sb_ref[...]
    kp = _rope_lanes(_dot(xkv, wkp_ref[...]), c, sa, sb)
    pos = i * tt + lax.broadcasted_iota(jnp.int32, (tt, 1), 0)
    lane = lax.broadcasted_iota(jnp.int32, (1, HEAD_SLOT), 1)
    blk_onehot = jnp.where(lane - HEAD_DIM == pos // L_SEL, 1.0, 0.0)
    for g in range(N_KV):
        ksel_ref[0, g] = (kp[:, g * HEAD_SLOT:(g + 1) * HEAD_SLOT] + blk_onehot).astype(BF16)
        kwin_ref[0, g] = kp[:, (N_KV + g) * HEAD_SLOT:(N_KV + g + 1) * HEAD_SLOT].astype(BF16)

    q = _rope_lanes(_dot(xq, wqp_ref[...]), c, sa, sb) * Q_SCALE_LOG2
    q_ref[0] = q.astype(BF16)
    gate_ref[0] = jax.nn.sigmoid(_dot_nt(wgt_ref[...], xq))


def _proj_prompt(h, g_kv, g_q, w_kvt, w_kpad, w_qpad, w_gt, rope_nat, rope_t, tt=512):
    B, T, D = h.shape
    const2 = lambda b, i: (0, 0)
    kv_t = jax.ShapeDtypeStruct((B, 2, N_KV, HEAD_DIM, T), F32)
    k_nat = jax.ShapeDtypeStruct((B, N_KV, T, HEAD_SLOT), BF16)
    v_t = jax.ShapeDtypeStruct((B, N_KV, T // LANE, HEAD_DIM, LANE), BF16)
    kv_spec = pl.BlockSpec((1, 2, N_KV, HEAD_DIM, tt), lambda b, i: (b, 0, 0, 0, i))
    k_spec = pl.BlockSpec((1, N_KV, tt, HEAD_SLOT), lambda b, i: (b, 0, i, 0))
    v_spec = pl.BlockSpec((1, N_KV, tt // LANE, HEAD_DIM, LANE), lambda b, i: (b, 0, i, 0, 0))
    return pl.pallas_call(
        functools.partial(_proj_prompt_kernel, tt=tt),
        grid=(B, T // tt),
        in_specs=[
            pl.BlockSpec((1, tt, D), lambda b, i: (b, i, 0)),
            pl.BlockSpec((1, D), const2), pl.BlockSpec((1, D), const2),
            pl.BlockSpec(w_kvt.shape, const2), pl.BlockSpec(w_kpad.shape, const2),
            pl.BlockSpec(w_qpad.shape, const2), pl.BlockSpec(w_gt.shape, const2),
            pl.BlockSpec((tt, LANE), lambda b, i: (i, 0)), pl.BlockSpec((tt, LANE), lambda b, i: (i, 0)),
            pl.BlockSpec((tt, LANE), lambda b, i: (i, 0)),
            pl.BlockSpec((ROT_HALF, tt), lambda b, i: (0, i)), pl.BlockSpec((ROT_HALF, tt), lambda b, i: (0, i)),
        ],
        out_specs=[kv_spec, kv_spec, kv_spec, k_spec, k_spec, v_spec, v_spec,
                   pl.BlockSpec((1, tt, N_HEADS * HEAD_SLOT), lambda b, i: (b, i, 0)),
                   pl.BlockSpec((1, GATE_LANES, tt), lambda b, i: (b, 0, i))],
        out_shape=[kv_t, kv_t, kv_t, k_nat, k_nat, v_t, v_t,
                   jax.ShapeDtypeStruct((B, T, N_HEADS * HEAD_SLOT), BF16),
                   jax.ShapeDtypeStruct((B, GATE_LANES, T), F32)],
        compiler_params=_cparams(("arbitrary", "arbitrary")),
        name="kv_q_proj_prompt",
    )(h, g_kv, g_q, w_kvt, w_kpad, w_qpad, w_gt, *rope_nat, *rope_t)


def _proj_sample_kernel(h_ref, gkv_ref, gq_ref, wkv_ref, wq_ref, wg_ref, c_ref, sa_ref, sb_ref,
                        cmp_ref, sel_ref, win_ref, q_ref, gate_ref):
    y = _rms_scale(h_ref[...])
    xkv = (y * gkv_ref[...]).astype(BF16)
    xq = (y * gq_ref[...]).astype(BF16)
    c, sa, sb = c_ref[...], sa_ref[...], sb_ref[...]
    kv = _dot(xkv, wkv_ref[...])
    for br, oref in enumerate((cmp_ref, sel_ref, win_ref)):
        oref[:, 0:KV_LANES] = _rope_lanes(kv[:, br * ROW_LANES:br * ROW_LANES + KV_LANES], c, sa, sb)
        oref[:, KV_LANES:ROW_LANES] = kv[:, br * ROW_LANES + KV_LANES:(br + 1) * ROW_LANES]
    q_ref[...] = (_rope_lanes(_dot(xq, wq_ref[...]), c, sa, sb) * (HEAD_DIM ** -0.5)).astype(BF16)
    gate_ref[...] = jax.nn.sigmoid(_dot(xq, wg_ref[...]))


def _proj_sample(h, g_kv, g_q, w_kv, w_q, w_g, rope_nat):
    M, D = h.shape
    return pl.pallas_call(
        _proj_sample_kernel,
        out_shape=[jax.ShapeDtypeStruct((M, ROW_LANES), F32)] * 3
                  + [jax.ShapeDtypeStruct((M, D), BF16), jax.ShapeDtypeStruct((M, GATE_LANES), F32)],
        compiler_params=pltpu.CompilerParams(vmem_limit_bytes=VMEM_LIMIT),
        name="kv_q_proj_sample",
    )(h, g_kv, g_q, w_kv, w_q, w_g, *rope_nat)


def _parts_kernel(*refs, pg, prefetch):
    refs = refs[prefetch:]
    x_refs, w_ref, out_ref, xs = refs[:pg], refs[pg], refs[pg + 1], refs[pg + 2]
    n_rows = pg * SEG_PER_PAGE
    for e in range(2):
        for gp in range(N_KV // 2):
            for p, xr in enumerate(x_refs):
                xt = jnp.concatenate([xr[0, e, 2 * gp], xr[0, e, 2 * gp + 1]], axis=0)
                xs[p * PAGE_SIZE:(p + 1) * PAGE_SIZE, :] = xt.T
            acc = jnp.zeros((n_rows, 2 * 2 * CMP_HIDDEN), F32)
            for rp in range(STRIDE // 2):
                a = xs[pl.ds(2 * rp, n_rows, stride=STRIDE), :]
                b = xs[pl.ds(2 * rp + 1, n_rows, stride=STRIDE), :]
                acc = acc + _dot(jnp.concatenate([a, b], axis=1).astype(BF16), w_ref[e, rp])
            c0 = (e * N_KV + 2 * gp) * 2 * CMP_HIDDEN
            out_ref[0, :, c0:c0 + 2 * 2 * CMP_HIDDEN] = acc


def _cmp_parts(kv_t, w1pair, pages=None, n_b=None, pg=16):
    blk = (1, 2, N_KV, HEAD_DIM, PAGE_SIZE)
    width = 2 * N_KV * 2 * CMP_HIDDEN
    if pages is None:
        n_b, ppb = kv_t.shape[0], kv_t.shape[-1] // PAGE_SIZE
        pg = min(pg, ppb)
        in_specs = [pl.BlockSpec(blk, functools.partial(lambda b, s, p: (b, 0, 0, 0, s * pg + p), p=p)) for p in range(pg)]
        in_specs.append(pl.BlockSpec(w1pair.shape, lambda b, s: (0, 0, 0, 0)))
        out_spec = pl.BlockSpec((1, pg * SEG_PER_PAGE, width), lambda b, s: (b, s, 0))
        grid_kw = dict(grid=(n_b, ppb // pg), in_specs=in_specs, out_specs=out_spec,
                       scratch_shapes=[pltpu.VMEM((pg * PAGE_SIZE, LANE), F32)])
        args = [kv_t] * pg + [w1pair]
        prefetch = 0
    else:
        ppb = pages.shape[0] // n_b
        pg = min(pg, ppb)
        in_specs = [pl.BlockSpec(blk, functools.partial(lambda b, s, pr, p: (pr[b * ppb + s * pg + p], 0, 0, 0, 0), p=p))
                    for p in range(pg)]
        in_specs.append(pl.BlockSpec(w1pair.shape, lambda b, s, pr: (0, 0, 0, 0)))
        out_spec = pl.BlockSpec((1, pg * SEG_PER_PAGE, width), lambda b, s, pr: (b, s, 0))
        grid_kw = dict(grid_spec=pltpu.PrefetchScalarGridSpec(
            num_scalar_prefetch=1, grid=(n_b, ppb // pg), in_specs=in_specs, out_specs=out_spec,
            scratch_shapes=[pltpu.VMEM((pg * PAGE_SIZE, LANE), F32)]))
        args = [pages] + [kv_t] * pg + [w1pair]
        prefetch = 1
    return pl.pallas_call(
        functools.partial(_parts_kernel, pg=pg, prefetch=prefetch),
        out_shape=jax.ShapeDtypeStruct((n_b, ppb * SEG_PER_PAGE, width), F32),
        compiler_params=_cparams(("arbitrary", "arbitrary")),
        name="cmp_parts",
        **grid_kw,
    )(*args)


def _cmp_finish_kernel(parts_ref, pe_ref, w1f_ref, w2p_ref, w2t_ref, knat_ref, kt_ref, vt_ref, *, n_seg):
    rows = lax.broadcasted_iota(jnp.int32, (n_seg, 1), 0)
    for e in range(2):
        pe_sum = _dot(pe_ref[e], w1f_ref[e])[0:1]
        for g in range(N_KV):
            c0 = (e * N_KV + g) * 2 * CMP_HIDDEN
            first = parts_ref[0, :, c0:c0 + CMP_HIDDEN]
            second = pltpu.roll(parts_ref[0, :, c0 + CMP_HIDDEN:c0 + 2 * CMP_HIDDEN], n_seg - 1, axis=0)
            act = jax.nn.gelu(first + second + pe_sum)
            act = jnp.where(rows < n_seg - 1, act, 0.0).astype(BF16)
            o_t = _dot_nt(w2t_ref[e], act).astype(BF16)
            if e == 0:
                knat_ref[0, g] = _dot(act, w2p_ref[...]).astype(BF16)
                kt_ref[0, g] = o_t
            else:
                vt_ref[0, g] = o_t


def _cmp_finish(parts, pe8, w1f, w2pad, w2t):
    n_b, n_seg, width = parts.shape
    c3 = lambda b: (0, 0, 0)
    return pl.pallas_call(
        functools.partial(_cmp_finish_kernel, n_seg=n_seg),
        grid=(n_b,),
        in_specs=[
            pl.BlockSpec((1, n_seg, width), lambda b: (b, 0, 0)),
            pl.BlockSpec(pe8.shape, c3), pl.BlockSpec(w1f.shape, c3),
            pl.BlockSpec(w2pad.shape, lambda b: (0, 0)), pl.BlockSpec(w2t.shape, c3),
        ],
        out_specs=[pl.BlockSpec((1, N_KV, n_seg, HEAD_SLOT), lambda b: (b, 0, 0, 0)),
                   pl.BlockSpec((1, N_KV, HEAD_DIM, n_seg), lambda b: (b, 0, 0, 0)),
                   pl.BlockSpec((1, N_KV, HEAD_DIM, n_seg), lambda b: (b, 0, 0, 0))],
        out_shape=[jax.ShapeDtypeStruct((n_b, N_KV, n_seg, HEAD_SLOT), BF16),
                   jax.ShapeDtypeStruct((n_b, N_KV, HEAD_DIM, n_seg), BF16),
                   jax.ShapeDtypeStruct((n_b, N_KV, HEAD_DIM, n_seg), BF16)],
        compiler_params=_cparams(("arbitrary",)),
        name="cmp_finish",
    )(parts, pe8, w1f, w2pad, w2t)


def _topk_rows(score, n_sel):
    n_blk = score.shape[0]
    n_tiles = n_blk // SUBLANE
    tiles = [score[r * SUBLANE:(r + 1) * SUBLANE] for r in range(n_tiles)]
    rank = [jnp.zeros(tiles[0].shape, F32) for _ in range(n_tiles)]
    sub = lax.broadcasted_iota(jnp.int32, (SUBLANE, 1), 0)
    for i in range(n_blk):
        bi = jnp.broadcast_to(score[i:i + 1], tiles[0].shape)
        ri = i // SUBLANE
        for r in range(n_tiles):
            if r > ri:
                ahead = jnp.where(bi >= tiles[r], 1.0, 0.0)
            elif r < ri:
                ahead = jnp.where(bi > tiles[r], 1.0, 0.0)
            else:
                tie = jnp.where(sub > i % SUBLANE, 1.0, 0.0)
                ahead = jnp.where(bi > tiles[r], 1.0, jnp.where(bi == tiles[r], tie, 0.0))
            rank[r] = rank[r] + ahead
    return jnp.concatenate(rank, axis=0) < float(n_sel)


def _attn_prompt_kernel(q_ref, gt_ref, kc_ref, vc_ref, ks_ref, vs_ref, kw_ref, vw_ref, impt_ref, eye_ref, o_ref,
                        *, tq, kc, t_len):
    i = pl.program_id(1)
    s0 = i * tq
    n_cmp = kc_ref.shape[2]
    n_blk = t_len // L_SEL
    rows = HPG * tq
    qpos = s0 + lax.broadcasted_iota(jnp.int32, (1, tq), 1)
    qpos4 = jnp.concatenate([qpos] * HPG, axis=1)
    blk = lax.broadcasted_iota(jnp.int32, (n_blk, 1), 0)
    cur = qpos // L_SEL
    valid = blk <= cur
    forced = (blk == 0) | (blk == cur) | (blk == cur - 1)
    c_end = lax.broadcasted_iota(jnp.int32, (n_cmp, 1), 0) * STRIDE + (L_CMP - 1)
    cmask = c_end <= qpos4
    slot_lane = lax.broadcasted_iota(jnp.int32, (1, HEAD_SLOT), 1)
    n_full = s0 // kc
    w_len = WINDOW + tq
    w0 = pl.multiple_of(jnp.clip(s0 - WINDOW, 0, t_len - w_len), LANE)
    wpos = w0 + lax.broadcasted_iota(jnp.int32, (w_len, 1), 0)
    wmask = (wpos <= qpos4) & (wpos >= qpos4 - WINDOW)
    gt = gt_ref[0]
    heads = []

    def flash_update(s, v, carry):
        m_i, l_i, acc = carry
        m_new = jnp.maximum(m_i, jnp.max(s, axis=0, keepdims=True))
        alpha = jnp.exp2(m_i - m_new)
        e = jnp.exp2(s - m_new)
        return m_new, alpha * l_i + jnp.sum(e, axis=0, keepdims=True), alpha * acc + _dot(v, e.astype(BF16))

    def gate_row(g, br):
        return jnp.concatenate([gt[(g * HPG + j) * N_BRANCH + br:(g * HPG + j) * N_BRANCH + br + 1] for j in range(HPG)], axis=1)

    for gp in range(N_KV // 2):
        gs = (2 * gp, 2 * gp + 1)
        qg = [jnp.concatenate([q_ref[0, :, (g * HPG + j) * HEAD_SLOT:(g * HPG + j + 1) * HEAD_SLOT] for j in range(HPG)], axis=0)
              for g in gs]

        s_c = [_dot_nt(kc_ref[0, g], qg[t]) for t, g in enumerate(gs)]
        p_c = [_masked_softmax(s, cmask, 0, base2=True) for s in s_c]
        o_c = [_dot(vc_ref[0, g], p_c[t].astype(BF16)) for t, g in enumerate(gs)]
        qa = []
        for t in range(2):
            p_sum = p_c[t][:, 0:tq]
            for j in range(1, HPG):
                p_sum = p_sum + p_c[t][:, j * tq:(j + 1) * tq]
            imp = _dot_exact_rhs(impt_ref[...], p_sum)
            score = jnp.where(forced, FORCE_SCORE, jnp.where(valid, imp, -1.0))
            sel = _topk_rows(score, min(N_SEL, n_blk)) & valid
            pieces = [jnp.ones((HEAD_DIM, tq), F32), jnp.where(sel, 1.0, 0.0)]
            if n_blk < HEAD_SLOT - HEAD_DIM:
                pieces.append(jnp.zeros((HEAD_SLOT - HEAD_DIM - n_blk, tq), F32))
            sel_pad = jnp.concatenate(pieces, axis=0)
            sel_q = _dot_nt(eye_ref[...], sel_pad.astype(BF16))
            bias = ((sel_q - 1.0) * -NEG_INF).astype(BF16)
            qa.append(jnp.where(slot_lane < HEAD_DIM, qg[t], jnp.concatenate([bias] * HPG, axis=0)))

        def scores(t, c):
            return _dot_nt(ks_ref[0, gs[t], pl.ds(pl.multiple_of(c * kc, kc), kc), :], qa[t])

        def values(t, c):
            return jnp.concatenate([vs_ref[0, gs[t], c * (kc // LANE) + u] for u in range(kc // LANE)], axis=1)

        def chunk(c, carry):
            s = [scores(0, c), scores(1, c)]
            return tuple(flash_update(s[t], values(t, c), carry[t]) for t in range(2))

        init1 = (jnp.full((1, rows), NEG_INF, F32), jnp.zeros((1, rows), F32), jnp.zeros((HEAD_DIM, rows), F32))
        carry = lax.fori_loop(0, n_full, chunk, (init1, init1))
        causal = (n_full * kc + lax.broadcasted_iota(jnp.int32, (kc, 1), 0)) <= qpos4
        s_d = [scores(0, n_full), scores(1, n_full)]
        last = [flash_update(jnp.where(causal, s_d[t], NEG_INF), values(t, n_full), carry[t]) for t in range(2)]
        o_s = [acc * (1.0 / l) for _, l, acc in last]

        s_w = [_dot_nt(kw_ref[0, g, pl.ds(w0, w_len), :], qg[t]) for t, g in enumerate(gs)]
        o_w = []
        for t, g in enumerate(gs):
            sm = jnp.where(wmask, s_w[t], NEG_INF)
            e = jnp.exp2(sm - jnp.max(sm, axis=0, keepdims=True))
            v_w = jnp.concatenate([vw_ref[0, g, w0 // LANE + u] for u in range(w_len // LANE)], axis=1)
            o_w.append(_dot(v_w, e.astype(BF16)) * (1.0 / jnp.sum(e, axis=0, keepdims=True)))

        for t, g in enumerate(gs):
            comb = gate_row(g, 0) * o_c[t] + gate_row(g, 1) * o_s[t] + gate_row(g, 2) * o_w[t]
            heads += [comb[:, j * tq:(j + 1) * tq] for j in range(HPG)]

    o_ref[0] = jnp.concatenate(heads, axis=0).T.astype(BF16)


def _attn_prompt(q_pad, gate_t, kc_nat, vc_t, ksel, vsel, kwin, vwin, imp_t, tq=256, kc=512):
    B, T, _ = q_pad.shape
    n_cmp = kc_nat.shape[2]
    eye = jnp.eye(tq, dtype=BF16)
    b4 = lambda b, i: (b, 0, 0, 0)
    b5 = lambda b, i: (b, 0, 0, 0, 0)
    return pl.pallas_call(
        functools.partial(_attn_prompt_kernel, tq=tq, kc=kc, t_len=T),
        grid=(B, T // tq),
        in_specs=[
            pl.BlockSpec((1, tq, N_HEADS * HEAD_SLOT), lambda b, i: (b, i, 0)),
            pl.BlockSpec((1, GATE_LANES, tq), lambda b, i: (b, 0, i)),
            pl.BlockSpec((1, N_KV, n_cmp, HEAD_SLOT), b4),
            pl.BlockSpec((1, N_KV, HEAD_DIM, n_cmp), b4),
            pl.BlockSpec((1, N_KV, T, HEAD_SLOT), b4),
            pl.BlockSpec((1, N_KV, T // LANE, HEAD_DIM, LANE), b5),
            pl.BlockSpec((1, N_KV, T, HEAD_SLOT), b4),
            pl.BlockSpec((1, N_KV, T // LANE, HEAD_DIM, LANE), b5),
            pl.BlockSpec(imp_t.shape, lambda b, i: (0, 0)),
            pl.BlockSpec(eye.shape, lambda b, i: (0, 0)),
        ],
        out_specs=pl.BlockSpec((1, tq, D_MODEL), lambda b, i: (b, i, 0)),
        out_shape=jax.ShapeDtypeStruct((B, T, D_MODEL), BF16),
        compiler_params=_cparams(("arbitrary", "arbitrary")),
        name="attn_prompt",
    )(q_pad, gate_t, kc_nat, vc_t, ksel, vsel, kwin, vwin, imp_t, eye)


def _sample_cmp_kernel(q_ref, kt_ref, vt_ref, imp_ref, oc_ref, idx_ref, val_ref, *, n_valid, cur, n_blk_lanes):
    n_cmp = kt_ref.shape[3]
    cmask = lax.broadcasted_iota(jnp.int32, (1, n_cmp), 1) < n_valid
    sums = []
    for g in range(N_KV):
        p = _masked_softmax(_dot(q_ref[0, g].astype(BF16), kt_ref[0, g]), cmask, -1)
        oc_ref[0, g] = _dot_nt(p.astype(BF16), vt_ref[0, g])
        sums.append(jnp.sum(p[0:HPG], axis=0, keepdims=True))
    imp = _dot_exact_lhs(jnp.concatenate(sums, axis=0), imp_ref[...])
    blk = lax.broadcasted_iota(jnp.int32, (1, n_blk_lanes), 1)
    blk_f = blk.astype(F32)
    valid = blk <= cur
    forced = (blk == 0) | (blk == cur) | (blk == cur - 1)
    score = jnp.where(forced, FORCE_SCORE, jnp.where(valid, imp, -1.0))
    score = jnp.where(valid, score, -2.0)
    out_lane = lax.broadcasted_iota(jnp.int32, (1, LANE), 1)
    idx_acc = jnp.zeros((N_KV, LANE), F32)
    val_acc = jnp.full((N_KV, LANE), -1.0, F32)
    for t in range(N_SEL):
        mx = jnp.max(score, axis=-1, keepdims=True)
        ix = jnp.min(jnp.where(score == mx, blk_f, float(n_blk_lanes)), axis=-1, keepdims=True)
        idx_acc = jnp.where(out_lane == t, ix, idx_acc)
        val_acc = jnp.where(out_lane == t, mx, val_acc)
        score = jnp.where(blk_f == ix, -3.0, score)
    idx_ref[0] = idx_acc.astype(jnp.int32)
    val_ref[0] = val_acc


def _sample_cmp(q4, kc_t, vc_t, imp_mat, n_valid, cur):
    DB = q4.shape[0]
    n_cmp = kc_t.shape[3]
    b4 = lambda b: (b, 0, 0, 0)
    return pl.pallas_call(
        functools.partial(_sample_cmp_kernel, n_valid=n_valid, cur=cur, n_blk_lanes=imp_mat.shape[1]),
        grid=(DB,),
        in_specs=[
            pl.BlockSpec((1, N_KV, JPAD, HEAD_DIM), b4),
            pl.BlockSpec((1, N_KV, HEAD_DIM, n_cmp), b4),
            pl.BlockSpec((1, N_KV, HEAD_DIM, n_cmp), b4),
            pl.BlockSpec(imp_mat.shape, lambda b: (0, 0)),
        ],
        out_specs=[
            pl.BlockSpec((1, N_KV, JPAD, HEAD_DIM), b4),
            pl.BlockSpec((1, N_KV, LANE), lambda b: (b, 0, 0)),
            pl.BlockSpec((1, N_KV, LANE), lambda b: (b, 0, 0)),
        ],
        out_shape=[jax.ShapeDtypeStruct((DB, N_KV, JPAD, HEAD_DIM), F32),
                   jax.ShapeDtypeStruct((DB, N_KV, LANE), jnp.int32),
                   jax.ShapeDtypeStruct((DB, N_KV, LANE), F32)],
        compiler_params=_cparams(("arbitrary",)),
        name="sample_cmp_topk",
    )(q4, kc_t, vc_t, imp_mat)


def _sample_sel_kernel(pg_ref, idx_ref, ok_ref, b0_ref, b1_ref, b2_ref, b3_ref, q_ref, gate_ref, selnew_ref,
                       win_ref, winnew_ref, oc_ref, o_ref, m_sc, l_sc, acc_sc, *, qpos, nb_past):
    del pg_ref
    b = pl.program_id(0)
    k = pl.program_id(1)
    bpp = PAGE_SIZE // L_SEL

    @pl.when(k == 0)
    def _():
        m_sc[...] = jnp.full(m_sc.shape, NEG_INF, F32)
        l_sc[...] = jnp.zeros_like(l_sc)
        acc_sc[...] = jnp.zeros_like(acc_sc)

    lane = lax.broadcasted_iota(jnp.int32, (1, PAGE_SIZE), 1)
    for g, bref in enumerate((b0_ref, b1_ref, b2_ref, b3_ref)):
        n = (b * N_KV + g) * N_SEL + k
        idx = idx_ref[n]
        tail = idx >= nb_past
        new_col = (lane + jnp.where(tail, 0, PAGE_SIZE)) == 0
        k_t = jnp.where(new_col, selnew_ref[0, 0, g], bref[0, 0, 0])
        v_t = jnp.where(new_col, selnew_ref[0, 1, g], bref[0, 1, 0])
        s = _dot(q_ref[0, g].astype(BF16), k_t.astype(BF16))
        first_pos = jnp.where(tail, idx * L_SEL, (idx // bpp) * PAGE_SIZE)
        half = jnp.where(tail, 0, idx % bpp)
        kpos = first_pos + lane + jnp.where(ok_ref[n] > 0, 0, qpos + 1)
        keep = (lane // L_SEL == half) & (kpos <= qpos)
        sm = jnp.where(keep, s, NEG_INF)
        m_new = jnp.maximum(m_sc[g], jnp.max(sm, axis=-1, keepdims=True))
        alpha = jnp.exp(m_sc[g] - m_new)
        e = jnp.where(keep, jnp.exp(sm - m_new), 0.0)
        l_sc[g] = alpha * l_sc[g] + jnp.sum(e, axis=-1, keepdims=True)
        acc_sc[g] = alpha * acc_sc[g] + _dot_nt(e.astype(BF16), v_t.astype(BF16))
        m_sc[g] = m_new

    @pl.when(k == pl.num_programs(1) - 1)
    def _():
        n_win = win_ref.shape[4]
        ext_lane = lax.broadcasted_iota(jnp.int32, (1, LANE), 1)
        wpos = qpos - n_win + lax.broadcasted_iota(jnp.int32, (1, n_win + LANE), 1)
        wmask = (wpos >= 0) & (wpos <= qpos) & (wpos >= qpos - WINDOW)
        for g in range(N_KV):
            k_t = jnp.concatenate([win_ref[0, 0, g], jnp.where(ext_lane == 0, winnew_ref[0, 0, g], 0.0)], axis=1)
            v_t = jnp.concatenate([win_ref[0, 1, g], jnp.where(ext_lane == 0, winnew_ref[0, 1, g], 0.0)], axis=1)
            p_w = _masked_softmax(_dot(q_ref[0, g].astype(BF16), k_t.astype(BF16)), wmask, -1)
            o_w = _dot_nt(p_w.astype(BF16), v_t.astype(BF16))
            gate = gate_ref[0, g]
            o_ref[0, g] = gate[:, 0:1] * oc_ref[0, g] + gate[:, 1:2] * (acc_sc[g] / l_sc[g]) + gate[:, 2:3] * o_w


def _sample_sel(pages, idx, ok, cache_t, q4, gate4, sel_new_t, win_t, win_new_t, o_c, qpos, nb_past):
    DB = q4.shape[0]
    n_win = win_t.shape[4]

    def page_spec(g):
        return pl.BlockSpec((1, 2, 1, HEAD_DIM, PAGE_SIZE),
                            lambda b, k, pg_ref, idx_ref, ok_ref: (pg_ref[(b * N_KV + g) * N_SEL + k], 0, g, 0, 0))

    b4 = lambda b, k, *_: (b, 0, 0, 0)
    b5 = lambda b, k, *_: (b, 0, 0, 0, 0)
    grid_spec = pltpu.PrefetchScalarGridSpec(
        num_scalar_prefetch=3,
        grid=(DB, N_SEL),
        in_specs=[page_spec(g) for g in range(N_KV)] + [
            pl.BlockSpec((1, N_KV, JPAD, HEAD_DIM), b4),
            pl.BlockSpec((1, N_KV, JPAD, N_BRANCH), b4),
            pl.BlockSpec((1, 2, N_KV, HEAD_DIM, 1), b5),
            pl.BlockSpec((1, 2, N_KV, HEAD_DIM, n_win), b5),
            pl.BlockSpec((1, 2, N_KV, HEAD_DIM, 1), b5),
            pl.BlockSpec((1, N_KV, JPAD, HEAD_DIM), b4),
        ],
        out_specs=pl.BlockSpec((1, N_KV, JPAD, HEAD_DIM), b4),
        scratch_shapes=[pltpu.VMEM((N_KV, JPAD, 1), F32), pltpu.VMEM((N_KV, JPAD, 1), F32),
                        pltpu.VMEM((N_KV, JPAD, HEAD_DIM), F32)],
    )
    return pl.pallas_call(
        functools.partial(_sample_sel_kernel, qpos=qpos, nb_past=nb_past),
        grid_spec=grid_spec,
        out_shape=jax.ShapeDtypeStruct((DB, N_KV, JPAD, HEAD_DIM), F32),
        compiler_params=_cparams(("arbitrary", "arbitrary")),
        name="sample_sel_win",
    )(pages, idx, ok, *([cache_t] * N_KV), q4, gate4, sel_new_t, win_t, win_new_t, o_c)


def _imp_matrix(n_cmp_rows, n_blk_cols):
    m = np.zeros((n_cmp_rows, n_blk_cols), np.float32)
    for j in range(n_blk_cols):
        for a in range(SEL_RATIO):
            for c in range(CMP_PARTS):
                i = SEL_RATIO * j + a - c
                if 0 <= i < n_cmp_rows:
                    m[i, j] += 1.0
    return m


def _rope_angles(pos):
    inv = ROPE_THETA ** (-jnp.arange(0, ROT_DIM, 2, dtype=F32) / ROT_DIM)
    ang = pos.astype(F32)[:, None] * inv[None, :]
    return jnp.cos(ang), jnp.sin(ang)


def _rope_tables(pos, period):
    cos, sin = _rope_angles(pos)
    n = pos.shape[0]
    rest1, rest0, zh = jnp.ones((n, period - ROT_DIM), F32), jnp.zeros((n, period - ROT_DIM), F32), jnp.zeros_like(sin)
    c = jnp.concatenate([cos, cos, rest1], axis=1)
    sa = jnp.concatenate([-sin, zh, rest0], axis=1)
    sb = jnp.concatenate([zh, sin, rest0], axis=1)
    return tuple(jnp.tile(t, (1, LANE // period)) for t in (c, sa, sb))


def _pad_heads(w, n_heads):
    w = w.reshape(w.shape[0], n_heads, HEAD_DIM)
    return jnp.pad(w, ((0, 0), (0, 0), (0, HEAD_SLOT - HEAD_DIM))).reshape(w.shape[0], n_heads * HEAD_SLOT)


def _to_rows(kv_t):
    return jnp.transpose(kv_t, (0, 4, 1, 2, 3))


def _to_cols(kv):
    return jnp.transpose(kv, (0, 2, 3, 4, 1))


def kernel(x_prompt, x_sample, state_pool, cache_cmp_kv, cache_sel_kv, state_win_kv, page_table,
           norm_mix, norm_ffn, pool_w, pool_scale, w_qg, w_o, norm_kv, w_kv, cmp_pe, cmp_w1, cmp_w2,
           mlp_up, mlp_down, norm_final):
    B, T, D = x_prompt.shape
    DB, S, _ = x_sample.shape
    past_len = page_table.shape[1] * PAGE_SIZE
    n_blk_p = T // L_SEL
    assert S == 1 and D == D_MODEL and T % PAGE_SIZE == 0 and n_blk_p <= HEAD_SLOT - HEAD_DIM and n_blk_p % SUBLANE == 0

    n_q = N_HEADS * HEAD_DIM
    w_q = w_qg[0][:, :n_q]
    w_gate = jnp.pad(w_qg[0][:, n_q:], ((0, 0), (0, GATE_LANES - N_HEADS * N_BRANCH)))
    w_kv5 = w_kv.reshape(D, N_BRANCH, 2, N_KV * HEAD_DIM)
    w_kpad = jnp.concatenate([_pad_heads(w_kv5[:, 1, 0], N_KV), _pad_heads(w_kv5[:, 2, 0], N_KV)], axis=1).astype(BF16)
    w_kv_b = w_kv.astype(BF16)
    w_o_b = w_o[0].astype(BF16)
    pool_w_b = pool_w[0].astype(BF16)
    up_b, down_b = mlp_up.astype(BF16), mlp_down.astype(BF16)
    g_mix0, g_mix1 = norm_mix[0:1], norm_mix[1:2]
    g_kv, g_fin = norm_kv[None, :], norm_final[None, :]
    w1_6 = cmp_w1.reshape(2, CMP_PARTS, STRIDE // 2, 2, HEAD_DIM, CMP_HIDDEN)
    w1pair = jnp.einsum('ab,eprshk->ersahbpk', jnp.eye(2, dtype=F32), w1_6)
    w1pair = w1pair.reshape(2, STRIDE // 2, 2 * 2 * HEAD_DIM, 2 * CMP_PARTS * CMP_HIDDEN).astype(BF16)
    w1_flat = cmp_w1.reshape(2, L_CMP * HEAD_DIM, CMP_HIDDEN).astype(BF16)
    pe8 = jnp.broadcast_to(cmp_pe.transpose(1, 0, 2).reshape(2, 1, L_CMP * HEAD_DIM), (2, SUBLANE, L_CMP * HEAD_DIM)).astype(BF16)
    w2pad = jnp.pad(cmp_w2[0], ((0, 0), (0, HEAD_SLOT - HEAD_DIM))).astype(BF16)
    w2t = cmp_w2.transpose(0, 2, 1).astype(BF16)

    h1, pool_tail = _pool_prompt(x_prompt, g_mix0, pool_w_b, pool_scale)
    pool_prompt = pool_tail[None, :, POOL_HALO - POOL_BUF:]
    h2 = _mlp(h1.reshape(B * T, D), norm_ffn[0:1], up_b[0], down_b[0], g_fin)
    pos_p = jnp.arange(T)
    cos_p, sin_p = _rope_angles(pos_p)
    cmp_t, sel_t, win_t, ksel, kwin, vsel, vwin, q_pad, gate_t = _proj_prompt(
        h2.reshape(B, T, D), g_kv, g_mix1, w_kv_b.T, w_kpad, _pad_heads(w_q, N_HEADS).astype(BF16), w_gate.T.astype(BF16),
        _rope_tables(pos_p, HEAD_SLOT), (cos_p.T, sin_p.T))
    kc_nat_p, _, vc_t_p = _cmp_finish(_cmp_parts(cmp_t, w1pair), pe8, w1_flat, w2pad, w2t)
    imp_t = jnp.asarray(_imp_matrix(kc_nat_p.shape[2], n_blk_p).T, BF16)
    o_p = _attn_prompt(q_pad, gate_t, kc_nat_p, vc_t_p, ksel, vsel, kwin, vwin, imp_t)
    y_p = _mlp(h2, norm_ffn[1:2], up_b[1], down_b[1], g_fin, o=o_p.reshape(B * T, D), w_o=w_o_b, final_norm=True)

    hist = jnp.concatenate([jnp.zeros((1, DB, D), F32), state_pool[0].transpose(1, 0, 2)], axis=0)
    hs1, us = _pool_sample(x_sample[:, 0], hist, g_mix0, pool_w_b, pool_scale)
    pool_sample = jnp.concatenate([state_pool[0][:, 1:], us[:, None]], axis=1)[None]
    hs2 = _mlp(hs1, norm_ffn[0:1], up_b[0], down_b[0], g_fin)
    cmp_s, sel_s, win_s, q_s, gate_s = _proj_sample(
        hs2, g_kv, g_mix1, w_kv_b, w_q.astype(BF16), w_gate.astype(BF16),
        _rope_tables(jnp.full((DB,), past_len, jnp.int32), HEAD_DIM))
    parts_s = _cmp_parts(_to_cols(cache_cmp_kv), w1pair, pages=page_table.reshape(-1), n_b=DB)
    _, kc_t_s, vc_t_s = _cmp_finish(parts_s, pe8, w1_flat, w2pad, w2t)
    n_valid = past_len // STRIDE - CMP_PARTS + 1
    nb_past = past_len // L_SEL
    n_blk_lanes = -(-(nb_past + 1) // LANE) * LANE
    jpad = ((0, 0), (0, 0), (0, JPAD - HPG), (0, 0))
    q4 = jnp.pad(q_s.astype(F32).reshape(DB, N_KV, HPG, HEAD_DIM), jpad)
    gate4 = jnp.pad(gate_s[:, :N_HEADS * N_BRANCH].reshape(DB, N_KV, HPG, N_BRANCH), jpad)
    imp_s = jnp.asarray(_imp_matrix(kc_t_s.shape[3], n_blk_lanes), BF16)
    o_c, idx_pad, val_pad = _sample_cmp(q4, kc_t_s, vc_t_s, imp_s, n_valid, nb_past)
    idx = idx_pad[:, :, :N_SEL]
    ok = (val_pad[:, :, :N_SEL] >= 0).astype(jnp.int32)
    bpp = PAGE_SIZE // L_SEL
    logical_page = (jnp.minimum(idx, nb_past - 1) // bpp).reshape(DB, -1)
    pages = jnp.take_along_axis(page_table, logical_page, axis=1)
    kv5 = lambda a, n, t: a.reshape(n, t, 2, N_KV, HEAD_DIM)
    o_s = _sample_sel(pages.reshape(-1), idx.reshape(-1), ok.reshape(-1), _to_cols(cache_sel_kv), q4, gate4,
                      _to_cols(kv5(sel_s, DB, 1)), _to_cols(state_win_kv), _to_cols(kv5(win_s, DB, 1)), o_c,
                      past_len, nb_past)
    o_s = o_s[:, :, :HPG].reshape(DB, D).astype(BF16)
    y_s = _mlp(hs2, norm_ffn[1:2], up_b[1], down_b[1], g_fin, o=o_s, w_o=w_o_b, final_norm=True)

    win_sample = jnp.concatenate([state_win_kv, kv5(win_s, DB, 1)], axis=1)[:, S:]
    return (y_p.reshape(B, T, D), y_s.reshape(DB, S, D),
            _to_rows(cmp_t), _to_rows(sel_t), _to_rows(win_t[..., T - min(WINDOW, T):]), pool_prompt,
            kv5(cmp_s, DB, 1), kv5(sel_s, DB, 1), win_sample, pool_sample)
```

```python
import functools
import math

import numpy as np
import jax
import jax.numpy as jnp
from jax import lax
from jax.experimental import pallas as pl
from jax.experimental.pallas import tpu as pltpu

D_MODEL = 1024
POOL_WINDOWS = (2, 4, 8, 16)
POOL_GROUP = D_MODEL // len(POOL_WINDOWS)
POOL_BUF = max(POOL_WINDOWS) - 1
POOL_HALO = 16
N_HEADS = 16
N_KV = 4
HPG = N_HEADS // N_KV
HEAD_DIM = 64
KV_LANES = N_KV * HEAD_DIM
ROW_LANES = 2 * KV_LANES
ROT_DIM = HEAD_DIM // 4
ROT_HALF = ROT_DIM // 2
ROPE_THETA = 500000.0
L_CMP = 32
STRIDE = 16
CMP_HIDDEN = 2 * HEAD_DIM
L_SEL = 64
SEL_RATIO = L_SEL // STRIDE
CMP_PARTS = L_CMP // STRIDE
N_SEL = 16
WINDOW = 512
N_BRANCH = 3
D_FF = 4 * D_MODEL
RMS_EPS = 1e-6
NEG_INF = -1e30
FORCE_SCORE = 1e9
Q_SCALE_LOG2 = HEAD_DIM ** -0.5 * math.log2(math.e)
PAGE_SIZE = 128
SEG_PER_PAGE = PAGE_SIZE // STRIDE
GATE_LANES = 128
LANE = 128
SUBLANE = 8
HEAD_SLOT = LANE
JPAD = SUBLANE
ONES_ROWS = 2 * SUBLANE

VMEM_LIMIT = 56 * 1024 * 1024

F32 = jnp.float32
BF16 = jnp.bfloat16


def _cparams(sem):
    return pltpu.CompilerParams(dimension_semantics=sem, vmem_limit_bytes=VMEM_LIMIT)


def _dot(a, b):
    return jnp.dot(a, b, preferred_element_type=F32)


def _dot_nt(a, b):
    return lax.dot_general(a, b, (((1,), (1,)), ((), ())), preferred_element_type=F32)


def _split3(x):
    hi = x.astype(BF16)
    r1 = x - hi.astype(F32)
    mid = r1.astype(BF16)
    lo = (r1 - mid.astype(F32)).astype(BF16)
    return hi, mid, lo


def _dot_exact_lhs(x, m_bf16):
    hi, mid, lo = _split3(x)
    return _dot(hi, m_bf16) + _dot(mid, m_bf16) + _dot(lo, m_bf16)


def _dot_exact_rhs(m_bf16, x):
    hi, mid, lo = _split3(x)
    return _dot(m_bf16, hi) + _dot(m_bf16, mid) + _dot(m_bf16, lo)


def _rms_scale(x):
    return x * lax.rsqrt(jnp.mean(x * x, axis=-1, keepdims=True) + RMS_EPS)


def _rope_lanes(x, c, sa, sb):
    outs = []
    for t in range(x.shape[1] // LANE):
        xc = x[:, t * LANE:(t + 1) * LANE]
        outs.append(xc * c + pltpu.roll(xc, LANE - ROT_HALF, axis=1) * sa + pltpu.roll(xc, ROT_HALF, axis=1) * sb)
    return jnp.concatenate(outs, axis=1)


def _masked_softmax(s, mask, axis, base2=False):
    sm = jnp.where(mask, s, NEG_INF)
    m = jnp.max(sm, axis=axis, keepdims=True)
    e = jnp.where(mask, (jnp.exp2 if base2 else jnp.exp)(sm - m), 0.0)
    l = jnp.sum(e, axis=axis, keepdims=True)
    return e * (1.0 / jnp.where(l > 0.0, l, 1.0))


def _pool_prompt_kernel(x_ref, xprev_ref, g_ref, w_ref, scale_ref, h_ref, buf_ref, *, tt):
    i = pl.program_id(1)
    x = x_ref[0]
    g = g_ref[...]
    u = _rms_scale(x) * g
    up = _rms_scale(xprev_ref[0]) * g
    up = jnp.where(i > 0, up, 0.0)
    ext = jnp.concatenate([up, u], axis=0)
    pos = i * tt + lax.broadcasted_iota(jnp.int32, (tt, 1), 0)
    scale = scale_ref[...]
    for gi, w in enumerate(POOL_WINDOWS):
        sl = slice(gi * POOL_GROUP, (gi + 1) * POOL_GROUP)
        s = ext[:, sl]
        k = 1
        while k < w:
            s = s + pltpu.roll(s, k, axis=0)
            k *= 2
        s = s[POOL_HALO:]
        cnt = jnp.minimum(pos + 1, w).astype(F32)
        d = s / cnt - u[:, sl]
        z = _dot(d.astype(BF16), w_ref[gi])
        h_ref[0, :, sl] = x[:, sl] + z * scale[:, sl]
    buf_ref[0] = u[tt - POOL_HALO:]


def _pool_prompt(x, g_mix, w_pool_bf16, scale, tt=512):
    B, T, D = x.shape
    hb = tt // POOL_HALO
    return pl.pallas_call(
        functools.partial(_pool_prompt_kernel, tt=tt),
        grid=(B, T // tt),
        in_specs=[
            pl.BlockSpec((1, tt, D), lambda b, i: (b, i, 0)),
            pl.BlockSpec((1, POOL_HALO, D), lambda b, i: (b, jnp.maximum(i * hb - 1, 0), 0)),
            pl.BlockSpec((1, D), lambda b, i: (0, 0)),
            pl.BlockSpec((len(POOL_WINDOWS), POOL_GROUP, POOL_GROUP), lambda b, i: (0, 0, 0)),
            pl.BlockSpec((1, D), lambda b, i: (0, 0)),
        ],
        out_specs=[
            pl.BlockSpec((1, tt, D), lambda b, i: (b, i, 0)),
            pl.BlockSpec((1, POOL_HALO, D), lambda b, i: (b, 0, 0)),
        ],
        out_shape=[jax.ShapeDtypeStruct((B, T, D), F32), jax.ShapeDtypeStruct((B, POOL_HALO, D), F32)],
        compiler_params=_cparams(("arbitrary", "arbitrary")),
        name="pool_prompt",
    )(x, x, g_mix, w_pool_bf16, scale)


def _pool_sample_kernel(x_ref, hist_ref, g_ref, w_ref, scale_ref, h_ref, u_ref):
    x = x_ref[...]
    u = _rms_scale(x) * g_ref[...]
    u_ref[...] = u
    scale = scale_ref[...]
    for gi, w in enumerate(POOL_WINDOWS):
        sl = slice(gi * POOL_GROUP, (gi + 1) * POOL_GROUP)
        s = u[:, sl]
        for r in range(1, w):
            s = s + hist_ref[POOL_BUF + 1 - r][:, sl]
        d = s / float(w) - u[:, sl]
        z = _dot(d.astype(BF16), w_ref[gi])
        h_ref[:, sl] = x[:, sl] + z * scale[:, sl]


def _pool_sample(x, hist, g_mix, w_pool_bf16, scale):
    DB, D = x.shape
    return pl.pallas_call(
        _pool_sample_kernel,
        out_shape=[jax.ShapeDtypeStruct((DB, D), F32), jax.ShapeDtypeStruct((DB, D), F32)],
        compiler_params=pltpu.CompilerParams(vmem_limit_bytes=VMEM_LIMIT),
        name="pool_sample",
    )(x, hist, g_mix, w_pool_bf16, scale)


def _mlp_kernel(*refs, has_attn, final_norm):
    if has_attn:
        h_ref, o_ref, wo_ref, g_ref, wup_ref, wdn_ref, gfin_ref, out_ref, hres, xn, acc = refs
    else:
        h_ref, g_ref, wup_ref, wdn_ref, gfin_ref, out_ref, hres, xn, acc = refs
    f = pl.program_id(1)

    @pl.when(f == 0)
    def _():
        h = h_ref[...]
        if has_attn:
            h = h + _dot(o_ref[...], wo_ref[...])
        hres[...] = h
        xn[...] = (_rms_scale(h) * g_ref[...]).astype(BF16)
        acc[...] = jnp.zeros_like(acc)

    a = jnp.maximum(_dot(xn[...], wup_ref[...]), 0.0)
    acc[...] += _dot((a * a).astype(BF16), wdn_ref[...])

    @pl.when(f == pl.num_programs(1) - 1)
    def _():
        y = hres[...] + acc[...]
        if final_norm:
            y = _rms_scale(y) * gfin_ref[...]
        out_ref[...] = y


def _mlp(h, g_ffn, w_up, w_down, g_final, o=None, w_o=None, final_norm=False, tm=1024, tf=1024):
    M, D = h.shape
    tm = min(tm, M)
    has_attn = o is not None
    row = lambda i, f: (i, 0)
    const = lambda i, f: (0, 0)
    in_specs = [pl.BlockSpec((tm, D), row)]
    args = [h]
    if has_attn:
        in_specs += [pl.BlockSpec((tm, D), row), pl.BlockSpec((D, D), const)]
        args += [o, w_o]
    in_specs += [
        pl.BlockSpec((1, D), const),
        pl.BlockSpec((D, tf), lambda i, f: (0, f)),
        pl.BlockSpec((tf, D), lambda i, f: (f, 0)),
        pl.BlockSpec((1, D), const),
    ]
    args += [g_ffn, w_up, w_down, g_final]
    return pl.pallas_call(
        functools.partial(_mlp_kernel, has_attn=has_attn, final_norm=final_norm),
        grid=(M // tm, D_FF // tf),
        in_specs=in_specs,
        out_specs=pl.BlockSpec((tm, D), row),
        out_shape=jax.ShapeDtypeStruct((M, D), F32),
        scratch_shapes=[pltpu.VMEM((tm, D), F32), pltpu.VMEM((tm, D), BF16), pltpu.VMEM((tm, D), F32)],
        compiler_params=_cparams(("arbitrary", "arbitrary")),
        name="mlp",
    )(*args)


def _proj_prompt_kernel(h_ref, gkv_ref, gq_ref, wkvt_ref, wkp_ref, wqp_ref, wgt_ref, c_ref, sa_ref, sb_ref, ct_ref, st_ref,
                        cmp_ref, sel_ref, win_ref, ksel_ref, kwin_ref, vsel_ref, vwin_ref, q_ref, gate_ref, *, tt):
    i = pl.program_id(1)
    y = _rms_scale(h_ref[0])
    xkv = (y * gkv_ref[...]).astype(BF16)
    xq = (y * gq_ref[...]).astype(BF16)

    kvt = _dot_nt(wkvt_ref[...], xkv)
    cos_t, sin_t = ct_ref[...], st_ref[...]
    for br, (oref, vref) in enumerate(((cmp_ref, None), (sel_ref, vsel_ref), (win_ref, vwin_ref))):
        base = br * ROW_LANES
        for g in range(N_KV):
            r0 = base + g * HEAD_DIM
            x1, x2 = kvt[r0:r0 + ROT_HALF], kvt[r0 + ROT_HALF:r0 + ROT_DIM]
            oref[0, 0, g] = jnp.concatenate(
                [x1 * cos_t - x2 * sin_t, x2 * cos_t + x1 * sin_t, kvt[r0 + ROT_DIM:r0 + HEAD_DIM]], axis=0)
            v = kvt[r0 + KV_LANES:r0 + KV_LANES + HEAD_DIM]
            oref[0, 1, g] = v
            if vref is not None:
                for c in range(tt // LANE):
                    vref[0, g, c] = v[:, c * LANE:(c + 1) * LANE].astype(BF16)

    c, sa, sb = c_ref[...], sa_ref[...], sb_ref[...]
    kp = _rope_lanes(_dot(xkv, wkp_ref[...]), c, sa, sb)
    pos = i * tt + lax.broadcasted_iota(jnp.int32, (tt, 1), 0)
    lane = lax.broadcasted_iota(jnp.int32, (1, HEAD_SLOT), 1)
    blk_onehot = jnp.where(lane - HEAD_DIM == pos // L_SEL, 1.0, 0.0)
    for g in range(N_KV):
        ksel_ref[0, g] = (kp[:, g * HEAD_SLOT:(g + 1) * HEAD_SLOT] + blk_onehot).astype(BF16)
        kwin_ref[0, g] = kp[:, (N_KV + g) * HEAD_SLOT:(N_KV + g + 1) * HEAD_SLOT].astype(BF16)

    q = _rope_lanes(_dot(xq, wqp_ref[...]), c, sa, sb) * Q_SCALE_LOG2
    q_ref[0] = q.astype(BF16)
    gate_ref[0] = jax.nn.sigmoid(_dot_nt(wgt_ref[...], xq))


def _proj_prompt(h, g_kv, g_q, w_kvt, w_kpad, w_qpad, w_gt, rope_nat, rope_t, tt=512):
    B, T, D = h.shape
    const2 = lambda b, i: (0, 0)
    kv_t = jax.ShapeDtypeStruct((B, 2, N_KV, HEAD_DIM, T), F32)
    k_nat = jax.ShapeDtypeStruct((B, N_KV, T, HEAD_SLOT), BF16)
    v_t = jax.ShapeDtypeStruct((B, N_KV, T // LANE, HEAD_DIM, LANE), BF16)
    kv_spec = pl.BlockSpec((1, 2, N_KV, HEAD_DIM, tt), lambda b, i: (b, 0, 0, 0, i))
    k_spec = pl.BlockSpec((1, N_KV, tt, HEAD_SLOT), lambda b, i: (b, 0, i, 0))
    v_spec = pl.BlockSpec((1, N_KV, tt // LANE, HEAD_DIM, LANE), lambda b, i: (b, 0, i, 0, 0))
    return pl.pallas_call(
        functools.partial(_proj_prompt_kernel, tt=tt),
        grid=(B, T // tt),
        in_specs=[
            pl.BlockSpec((1, tt, D), lambda b, i: (b, i, 0)),
            pl.BlockSpec((1, D), const2), pl.BlockSpec((1, D), const2),
            pl.BlockSpec(w_kvt.shape, const2), pl.BlockSpec(w_kpad.shape, const2),
            pl.BlockSpec(w_qpad.shape, const2), pl.BlockSpec(w_gt.shape, const2),
            pl.BlockSpec((tt, LANE), lambda b, i: (i, 0)), pl.BlockSpec((tt, LANE), lambda b, i: (i, 0)),
            pl.BlockSpec((tt, LANE), lambda b, i: (i, 0)),
            pl.BlockSpec((ROT_HALF, tt), lambda b, i: (0, i)), pl.BlockSpec((ROT_HALF, tt), lambda b, i: (0, i)),
        ],
        out_specs=[kv_spec, kv_spec, kv_spec, k_spec, k_spec, v_spec, v_spec,
                   pl.BlockSpec((1, tt, N_HEADS * HEAD_SLOT), lambda b, i: (b, i, 0)),
                   pl.BlockSpec((1, GATE_LANES, tt), lambda b, i: (b, 0, i))],
        out_shape=[kv_t, kv_t, kv_t, k_nat, k_nat, v_t, v_t,
                   jax.ShapeDtypeStruct((B, T, N_HEADS * HEAD_SLOT), BF16),
                   jax.ShapeDtypeStruct((B, GATE_LANES, T), F32)],
        compiler_params=_cparams(("arbitrary", "arbitrary")),
        name="kv_q_proj_prompt",
    )(h, g_kv, g_q, w_kvt, w_kpad, w_qpad, w_gt, *rope_nat, *rope_t)


def _proj_sample_kernel(h_ref, gkv_ref, gq_ref, wkv_ref, wq_ref, wg_ref, c_ref, sa_ref, sb_ref,
                        cmp_ref, sel_ref, win_ref, q_ref, gate_ref):
    y = _rms_scale(h_ref[...])
    xkv = (y * gkv_ref[...]).astype(BF16)
    xq = (y * gq_ref[...]).astype(BF16)
    c, sa, sb = c_ref[...], sa_ref[...], sb_ref[...]
    kv = _dot(xkv, wkv_ref[...])
    for br, oref in enumerate((cmp_ref, sel_ref, win_ref)):
        oref[:, 0:KV_LANES] = _rope_lanes(kv[:, br * ROW_LANES:br * ROW_LANES + KV_LANES], c, sa, sb)
        oref[:, KV_LANES:ROW_LANES] = kv[:, br * ROW_LANES + KV_LANES:(br + 1) * ROW_LANES]
    q_ref[...] = (_rope_lanes(_dot(xq, wq_ref[...]), c, sa, sb) * (HEAD_DIM ** -0.5)).astype(BF16)
    gate_ref[...] = jax.nn.sigmoid(_dot(xq, wg_ref[...]))


def _proj_sample(h, g_kv, g_q, w_kv, w_q, w_g, rope_nat):
    M, D = h.shape
    return pl.pallas_call(
        _proj_sample_kernel,
        out_shape=[jax.ShapeDtypeStruct((M, ROW_LANES), F32)] * 3
                  + [jax.ShapeDtypeStruct((M, D), BF16), jax.ShapeDtypeStruct((M, GATE_LANES), F32)],
        compiler_params=pltpu.CompilerParams(vmem_limit_bytes=VMEM_LIMIT),
        name="kv_q_proj_sample",
    )(h, g_kv, g_q, w_kv, w_q, w_g, *rope_nat)


def _parts_kernel(*refs, pg, prefetch):
    refs = refs[prefetch:]
    x_refs, w_ref, out_ref, xs = refs[:pg], refs[pg], refs[pg + 1], refs[pg + 2]
    n_rows = pg * SEG_PER_PAGE
    stages = [(e, gp) for e in range(2) for gp in range(N_KV // 2)]

    def fill(k):
        e, gp = stages[k]
        for p, xr in enumerate(x_refs):
            xt = jnp.concatenate([xr[0, e, 2 * gp], xr[0, e, 2 * gp + 1]], axis=0)
            xs[k % 2, p * PAGE_SIZE:(p + 1) * PAGE_SIZE, :] = xt.T

    fill(0)
    for k, (e, gp) in enumerate(stages):
        if k + 1 < len(stages):
            fill(k + 1)
        acc = jnp.zeros((n_rows, 2 * 2 * CMP_HIDDEN), F32)
        for rp in range(STRIDE // 2):
            a = xs.at[k % 2][pl.ds(2 * rp, n_rows, stride=STRIDE), :]
            b = xs.at[k % 2][pl.ds(2 * rp + 1, n_rows, stride=STRIDE), :]
            acc = acc + _dot(jnp.concatenate([a, b], axis=1).astype(BF16), w_ref[e, rp])
        c0 = (e * N_KV + 2 * gp) * 2 * CMP_HIDDEN
        out_ref[0, :, c0:c0 + 2 * 2 * CMP_HIDDEN] = acc


def _cmp_parts(kv_t, w1pair, pages=None, n_b=None, pg=16):
    blk = (1, 2, N_KV, HEAD_DIM, PAGE_SIZE)
    width = 2 * N_KV * 2 * CMP_HIDDEN
    if pages is None:
        n_b, ppb = kv_t.shape[0], kv_t.shape[-1] // PAGE_SIZE
        pg = min(pg, ppb)
        in_specs = [pl.BlockSpec(blk, functools.partial(lambda b, s, p: (b, 0, 0, 0, s * pg + p), p=p)) for p in range(pg)]
        in_specs.append(pl.BlockSpec(w1pair.shape, lambda b, s: (0, 0, 0, 0)))
        out_spec = pl.BlockSpec((1, pg * SEG_PER_PAGE, width), lambda b, s: (b, s, 0))
        grid_kw = dict(grid=(n_b, ppb // pg), in_specs=in_specs, out_specs=out_spec,
                       scratch_shapes=[pltpu.VMEM((2, pg * PAGE_SIZE, LANE), F32)])
        args = [kv_t] * pg + [w1pair]
        prefetch = 0
    else:
        ppb = pages.shape[0] // n_b
        pg = min(pg, ppb)
        in_specs = [pl.BlockSpec(blk, functools.partial(lambda b, s, pr, p: (pr[b * ppb + s * pg + p], 0, 0, 0, 0), p=p))
                    for p in range(pg)]
        in_specs.append(pl.BlockSpec(w1pair.shape, lambda b, s, pr: (0, 0, 0, 0)))
        out_spec = pl.BlockSpec((1, pg * SEG_PER_PAGE, width), lambda b, s, pr: (b, s, 0))
        grid_kw = dict(grid_spec=pltpu.PrefetchScalarGridSpec(
            num_scalar_prefetch=1, grid=(n_b, ppb // pg), in_specs=in_specs, out_specs=out_spec,
            scratch_shapes=[pltpu.VMEM((2, pg * PAGE_SIZE, LANE), F32)]))
        args = [pages] + [kv_t] * pg + [w1pair]
        prefetch = 1
    return pl.pallas_call(
        functools.partial(_parts_kernel, pg=pg, prefetch=prefetch),
        out_shape=jax.ShapeDtypeStruct((n_b, ppb * SEG_PER_PAGE, width), F32),
        compiler_params=_cparams(("arbitrary", "arbitrary")),
        name="cmp_parts",
        **grid_kw,
    )(*args)


def _cmp_finish_kernel(parts_ref, pe_ref, w1f_ref, w2p_ref, w2t_ref, knat_ref, kt_ref, vt_ref, *, n_seg):
    rows = lax.broadcasted_iota(jnp.int32, (n_seg, 1), 0)
    for e in range(2):
        pe_sum = _dot(pe_ref[e], w1f_ref[e])[0:1]
        for g in range(N_KV):
            c0 = (e * N_KV + g) * 2 * CMP_HIDDEN
            first = parts_ref[0, :, c0:c0 + CMP_HIDDEN]
            second = pltpu.roll(parts_ref[0, :, c0 + CMP_HIDDEN:c0 + 2 * CMP_HIDDEN], n_seg - 1, axis=0)
            act = jax.nn.gelu(first + second + pe_sum)
            act = jnp.where(rows < n_seg - 1, act, 0.0).astype(BF16)
            o_t = _dot_nt(w2t_ref[e], act).astype(BF16)
            if e == 0:
                knat_ref[0, g] = _dot(act, w2p_ref[...]).astype(BF16)
                kt_ref[0, g] = o_t
            else:
                vt_ref[0, g] = o_t


def _cmp_finish(parts, pe8, w1f, w2pad, w2t):
    n_b, n_seg, width = parts.shape
    c3 = lambda b: (0, 0, 0)
    return pl.pallas_call(
        functools.partial(_cmp_finish_kernel, n_seg=n_seg),
        grid=(n_b,),
        in_specs=[
            pl.BlockSpec((1, n_seg, width), lambda b: (b, 0, 0)),
            pl.BlockSpec(pe8.shape, c3), pl.BlockSpec(w1f.shape, c3),
            pl.BlockSpec(w2pad.shape, lambda b: (0, 0)), pl.BlockSpec(w2t.shape, c3),
        ],
        out_specs=[pl.BlockSpec((1, N_KV, n_seg, HEAD_SLOT), lambda b: (b, 0, 0, 0)),
                   pl.BlockSpec((1, N_KV, HEAD_DIM, n_seg), lambda b: (b, 0, 0, 0)),
                   pl.BlockSpec((1, N_KV, HEAD_DIM, n_seg), lambda b: (b, 0, 0, 0))],
        out_shape=[jax.ShapeDtypeStruct((n_b, N_KV, n_seg, HEAD_SLOT), BF16),
                   jax.ShapeDtypeStruct((n_b, N_KV, HEAD_DIM, n_seg), BF16),
                   jax.ShapeDtypeStruct((n_b, N_KV, HEAD_DIM, n_seg), BF16)],
        compiler_params=_cparams(("arbitrary",)),
        name="cmp_finish",
    )(parts, pe8, w1f, w2pad, w2t)


def _topk_rows(score, n_sel):
    n_blk = score.shape[0]
    n_tiles = n_blk // SUBLANE
    tiles = [score[r * SUBLANE:(r + 1) * SUBLANE] for r in range(n_tiles)]
    rank = [jnp.zeros(tiles[0].shape, F32) for _ in range(n_tiles)]
    sub = lax.broadcasted_iota(jnp.int32, (SUBLANE, 1), 0)
    for i in range(n_blk):
        bi = jnp.broadcast_to(score[i:i + 1], tiles[0].shape)
        ri = i // SUBLANE
        for r in range(n_tiles):
            if r > ri:
                ahead = jnp.where(bi >= tiles[r], 1.0, 0.0)
            elif r < ri:
                ahead = jnp.where(bi > tiles[r], 1.0, 0.0)
            else:
                tie = jnp.where(sub > i % SUBLANE, 1.0, 0.0)
                ahead = jnp.where(bi > tiles[r], 1.0, jnp.where(bi == tiles[r], tie, 0.0))
            rank[r] = rank[r] + ahead
    return jnp.concatenate(rank, axis=0) < float(n_sel)


def _attn_prompt_kernel(q_ref, gt_ref, kc_ref, vc_ref, ks_ref, vs_ref, kw_ref, vw_ref, impt_ref, eye_ref, o_ref,
                        *, tq, kc, t_len):
    i = pl.program_id(1)
    s0 = i * tq
    n_cmp = kc_ref.shape[2]
    n_blk = t_len // L_SEL
    rows = HPG * tq
    qpos = s0 + lax.broadcasted_iota(jnp.int32, (1, tq), 1)
    qpos4 = jnp.concatenate([qpos] * HPG, axis=1)
    blk = lax.broadcasted_iota(jnp.int32, (n_blk, 1), 0)
    cur = qpos // L_SEL
    valid = blk <= cur
    forced = (blk == 0) | (blk == cur) | (blk == cur - 1)
    c_end = lax.broadcasted_iota(jnp.int32, (n_cmp, 1), 0) * STRIDE + (L_CMP - 1)
    cmask = c_end <= qpos4
    slot_lane = lax.broadcasted_iota(jnp.int32, (1, HEAD_SLOT), 1)
    n_full = s0 // kc
    w_len = WINDOW + tq
    w0 = pl.multiple_of(jnp.clip(s0 - WINDOW, 0, t_len - w_len), LANE)
    wpos = w0 + lax.broadcasted_iota(jnp.int32, (w_len, 1), 0)
    wmask = (wpos <= qpos4) & (wpos >= qpos4 - WINDOW)
    gt = gt_ref[0]
    heads = []

    def with_ones(v):
        return jnp.concatenate([v, jnp.ones((ONES_ROWS, v.shape[1]), BF16)], axis=0)

    def flash_update(s, v, carry):
        m_i, acc = carry
        m_new = jnp.maximum(m_i, jnp.max(s, axis=0, keepdims=True))
        e = jnp.exp2(s - m_new)
        return m_new, jnp.exp2(m_i - m_new) * acc + _dot(v, e.astype(BF16))

    def normalise(acc):
        return acc[0:HEAD_DIM] * (1.0 / acc[HEAD_DIM:HEAD_DIM + 1])

    def gate_row(g, br):
        return jnp.concatenate([gt[(g * HPG + j) * N_BRANCH + br:(g * HPG + j) * N_BRANCH + br + 1] for j in range(HPG)], axis=1)

    for gp in range(N_KV // 2):
        gs = (2 * gp, 2 * gp + 1)
        qg = [jnp.concatenate([q_ref[0, :, (g * HPG + j) * HEAD_SLOT:(g * HPG + j + 1) * HEAD_SLOT] for j in range(HPG)], axis=0)
              for g in gs]

        s_c = [_dot_nt(kc_ref[0, g], qg[t]) for t, g in enumerate(gs)]
        p_c = [_masked_softmax(s, cmask, 0, base2=True) for s in s_c]
        o_c = [_dot(vc_ref[0, g], p_c[t].astype(BF16)) for t, g in enumerate(gs)]
        qa = []
        for t in range(2):
            p_sum = p_c[t][:, 0:tq]
            for j in range(1, HPG):
                p_sum = p_sum + p_c[t][:, j * tq:(j + 1) * tq]
            imp = _dot_exact_rhs(impt_ref[...], p_sum)
            score = jnp.where(forced, FORCE_SCORE, jnp.where(valid, imp, -1.0))
            sel = _topk_rows(score, min(N_SEL, n_blk)) & valid
            pieces = [jnp.ones((HEAD_DIM, tq), F32), jnp.where(sel, 1.0, 0.0)]
            if n_blk < HEAD_SLOT - HEAD_DIM:
                pieces.append(jnp.zeros((HEAD_SLOT - HEAD_DIM - n_blk, tq), F32))
            sel_pad = jnp.concatenate(pieces, axis=0)
            sel_q = _dot_nt(eye_ref[...], sel_pad.astype(BF16))
            bias = ((sel_q - 1.0) * -NEG_INF).astype(BF16)
            qa.append(jnp.where(slot_lane < HEAD_DIM, qg[t], jnp.concatenate([bias] * HPG, axis=0)))

        def scores(t, c):
            return _dot_nt(ks_ref[0, gs[t], pl.ds(pl.multiple_of(c * kc, kc), kc), :], qa[t])

        def values(t, c):
            return with_ones(jnp.concatenate([vs_ref[0, gs[t], c * (kc // LANE) + u] for u in range(kc // LANE)], axis=1))

        def chunk(c, carry):
            s = [scores(0, c), scores(1, c)]
            return tuple(flash_update(s[t], values(t, c), carry[t]) for t in range(2))

        init1 = (jnp.full((1, rows), NEG_INF, F32), jnp.zeros((HEAD_DIM + ONES_ROWS, rows), F32))
        carry = lax.fori_loop(0, n_full, chunk, (init1, init1))
        causal = (n_full * kc + lax.broadcasted_iota(jnp.int32, (kc, 1), 0)) <= qpos4
        s_d = [scores(0, n_full), scores(1, n_full)]
        last = [flash_update(jnp.where(causal, s_d[t], NEG_INF), values(t, n_full), carry[t]) for t in range(2)]
        o_s = [normalise(acc) for _, acc in last]

        s_w = [_dot_nt(kw_ref[0, g, pl.ds(w0, w_len), :], qg[t]) for t, g in enumerate(gs)]
        o_w = []
        for t, g in enumerate(gs):
            sm = jnp.where(wmask, s_w[t], NEG_INF)
            e = jnp.exp2(sm - jnp.max(sm, axis=0, keepdims=True))
            v_w = with_ones(jnp.concatenate([vw_ref[0, g, w0 // LANE + u] for u in range(w_len // LANE)], axis=1))
            o_w.append(normalise(_dot(v_w, e.astype(BF16))))

        for t, g in enumerate(gs):
            comb = gate_row(g, 0) * o_c[t] + gate_row(g, 1) * o_s[t] + gate_row(g, 2) * o_w[t]
            heads += [comb[:, j * tq:(j + 1) * tq] for j in range(HPG)]

    o_ref[0] = jnp.concatenate(heads, axis=0).T.astype(BF16)


def _attn_prompt(q_pad, gate_t, kc_nat, vc_t, ksel, vsel, kwin, vwin, imp_t, tq=256, kc=512):
    B, T, _ = q_pad.shape
    n_cmp = kc_nat.shape[2]
    eye = jnp.eye(tq, dtype=BF16)
    b4 = lambda b, i: (b, 0, 0, 0)
    b5 = lambda b, i: (b, 0, 0, 0, 0)
    return pl.pallas_call(
        functools.partial(_attn_prompt_kernel, tq=tq, kc=kc, t_len=T),
        grid=(B, T // tq),
        in_specs=[
            pl.BlockSpec((1, tq, N_HEADS * HEAD_SLOT), lambda b, i: (b, i, 0)),
            pl.BlockSpec((1, GATE_LANES, tq), lambda b, i: (b, 0, i)),
            pl.BlockSpec((1, N_KV, n_cmp, HEAD_SLOT), b4),
            pl.BlockSpec((1, N_KV, HEAD_DIM, n_cmp), b4),
            pl.BlockSpec((1, N_KV, T, HEAD_SLOT), b4),
            pl.BlockSpec((1, N_KV, T // LANE, HEAD_DIM, LANE), b5),
            pl.BlockSpec((1, N_KV, T, HEAD_SLOT), b4),
            pl.BlockSpec((1, N_KV, T // LANE, HEAD_DIM, LANE), b5),
            pl.BlockSpec(imp_t.shape, lambda b, i: (0, 0)),
            pl.BlockSpec(eye.shape, lambda b, i: (0, 0)),
        ],
        out_specs=pl.BlockSpec((1, tq, D_MODEL), lambda b, i: (b, i, 0)),
        out_shape=jax.ShapeDtypeStruct((B, T, D_MODEL), BF16),
        compiler_params=_cparams(("arbitrary", "arbitrary")),
        name="attn_prompt",
    )(q_pad, gate_t, kc_nat, vc_t, ksel, vsel, kwin, vwin, imp_t, eye)


def _sample_cmp_kernel(q_ref, kt_ref, vt_ref, imp_ref, oc_ref, idx_ref, val_ref, *, n_valid, cur, n_blk_lanes):
    n_cmp = kt_ref.shape[3]
    cmask = lax.broadcasted_iota(jnp.int32, (1, n_cmp), 1) < n_valid
    sums = []
    for g in range(N_KV):
        p = _masked_softmax(_dot(q_ref[0, g].astype(BF16), kt_ref[0, g]), cmask, -1)
        oc_ref[0, g] = _dot_nt(p.astype(BF16), vt_ref[0, g])
        sums.append(jnp.sum(p[0:HPG], axis=0, keepdims=True))
    imp = _dot_exact_lhs(jnp.concatenate(sums, axis=0), imp_ref[...])
    blk = lax.broadcasted_iota(jnp.int32, (1, n_blk_lanes), 1)
    blk_f = blk.astype(F32)
    valid = blk <= cur
    forced = (blk == 0) | (blk == cur) | (blk == cur - 1)
    score = jnp.where(forced, FORCE_SCORE, jnp.where(valid, imp, -1.0))
    score = jnp.where(valid, score, -2.0)
    out_lane = lax.broadcasted_iota(jnp.int32, (1, LANE), 1)
    idx_acc = jnp.zeros((N_KV, LANE), F32)
    val_acc = jnp.full((N_KV, LANE), -1.0, F32)
    for t in range(N_SEL):
        mx = jnp.max(score, axis=-1, keepdims=True)
        ix = jnp.min(jnp.where(score == mx, blk_f, float(n_blk_lanes)), axis=-1, keepdims=True)
        idx_acc = jnp.where(out_lane == t, ix, idx_acc)
        val_acc = jnp.where(out_lane == t, mx, val_acc)
        score = jnp.where(blk_f == ix, -3.0, score)
    idx_ref[0] = idx_acc.astype(jnp.int32)
    val_ref[0] = val_acc


def _sample_cmp(q4, kc_t, vc_t, imp_mat, n_valid, cur):
    DB = q4.shape[0]
    n_cmp = kc_t.shape[3]
    b4 = lambda b: (b, 0, 0, 0)
    return pl.pallas_call(
        functools.partial(_sample_cmp_kernel, n_valid=n_valid, cur=cur, n_blk_lanes=imp_mat.shape[1]),
        grid=(DB,),
        in_specs=[
            pl.BlockSpec((1, N_KV, JPAD, HEAD_DIM), b4),
            pl.BlockSpec((1, N_KV, HEAD_DIM, n_cmp), b4),
            pl.BlockSpec((1, N_KV, HEAD_DIM, n_cmp), b4),
            pl.BlockSpec(imp_mat.shape, lambda b: (0, 0)),
        ],
        out_specs=[
            pl.BlockSpec((1, N_KV, JPAD, HEAD_DIM), b4),
            pl.BlockSpec((1, N_KV, LANE), lambda b: (b, 0, 0)),
            pl.BlockSpec((1, N_KV, LANE), lambda b: (b, 0, 0)),
        ],
        out_shape=[jax.ShapeDtypeStruct((DB, N_KV, JPAD, HEAD_DIM), F32),
                   jax.ShapeDtypeStruct((DB, N_KV, LANE), jnp.int32),
                   jax.ShapeDtypeStruct((DB, N_KV, LANE), F32)],
        compiler_params=_cparams(("arbitrary",)),
        name="sample_cmp_topk",
    )(q4, kc_t, vc_t, imp_mat)


def _sample_sel_kernel(pg_ref, idx_ref, ok_ref, *refs, qpos, nb_past, kstep):
    del pg_ref
    page_refs = refs[:N_KV * kstep]
    q_ref, gate_ref, selnew_ref, win_ref, winnew_ref, oc_ref, o_ref, m_sc, l_sc, acc_sc = refs[N_KV * kstep:]
    b = pl.program_id(0)
    k = pl.program_id(1)
    bpp = PAGE_SIZE // L_SEL

    @pl.when(k == 0)
    def _():
        m_sc[...] = jnp.full(m_sc.shape, NEG_INF, F32)
        l_sc[...] = jnp.zeros_like(l_sc)
        acc_sc[...] = jnp.zeros_like(acc_sc)

    lane = lax.broadcasted_iota(jnp.int32, (1, PAGE_SIZE), 1)
    for g in range(N_KV):
        k_parts, v_parts, keep_parts = [], [], []
        for kk in range(kstep):
            bref = page_refs[g * kstep + kk]
            n = (b * N_KV + g) * N_SEL + k * kstep + kk
            idx = idx_ref[n]
            tail = idx >= nb_past
            new_col = (lane + jnp.where(tail, 0, PAGE_SIZE)) == 0
            k_parts.append(jnp.where(new_col, selnew_ref[0, 0, g], bref[0, 0, 0]).astype(BF16))
            v_parts.append(jnp.where(new_col, selnew_ref[0, 1, g], bref[0, 1, 0]).astype(BF16))
            first_pos = jnp.where(tail, idx * L_SEL, (idx // bpp) * PAGE_SIZE)
            half = jnp.where(tail, 0, idx % bpp)
            kpos = first_pos + lane + jnp.where(ok_ref[n] > 0, 0, qpos + 1)
            keep_parts.append(jnp.where((lane // L_SEL == half) & (kpos <= qpos), 1.0, 0.0))
        keep = jnp.concatenate(keep_parts, axis=1) > 0.5
        v_t = jnp.concatenate(v_parts, axis=1)
        s = _dot(q_ref[0, g].astype(BF16), jnp.concatenate(k_parts, axis=1))
        sm = jnp.where(keep, s, NEG_INF)
        m_new = jnp.maximum(m_sc[g], jnp.max(sm, axis=-1, keepdims=True))
        alpha = jnp.exp(m_sc[g] - m_new)
        e = jnp.where(keep, jnp.exp(sm - m_new), 0.0)
        l_sc[g] = alpha * l_sc[g] + jnp.sum(e, axis=-1, keepdims=True)
        acc_sc[g] = alpha * acc_sc[g] + _dot_nt(e.astype(BF16), v_t)
        m_sc[g] = m_new

    @pl.when(k == pl.num_programs(1) - 1)
    def _():
        n_win = win_ref.shape[4]
        ext_lane = lax.broadcasted_iota(jnp.int32, (1, LANE), 1)
        wpos = qpos - n_win + lax.broadcasted_iota(jnp.int32, (1, n_win + LANE), 1)
        wmask = (wpos >= 0) & (wpos <= qpos) & (wpos >= qpos - WINDOW)
        for g in range(N_KV):
            k_t = jnp.concatenate([win_ref[0, 0, g], jnp.where(ext_lane == 0, winnew_ref[0, 0, g], 0.0)], axis=1)
            v_t = jnp.concatenate([win_ref[0, 1, g], jnp.where(ext_lane == 0, winnew_ref[0, 1, g], 0.0)], axis=1)
            p_w = _masked_softmax(_dot(q_ref[0, g].astype(BF16), k_t.astype(BF16)), wmask, -1)
            o_w = _dot_nt(p_w.astype(BF16), v_t.astype(BF16))
            gate = gate_ref[0, g]
            o_ref[0, g] = gate[:, 0:1] * oc_ref[0, g] + gate[:, 1:2] * (acc_sc[g] / l_sc[g]) + gate[:, 2:3] * o_w


def _sample_sel(pages, idx, ok, cache_t, q4, gate4, sel_new_t, win_t, win_new_t, o_c, qpos, nb_past, kstep=N_SEL):
    DB = q4.shape[0]
    n_win = win_t.shape[4]

    def page_spec(g, kk):
        return pl.BlockSpec((1, 2, 1, HEAD_DIM, PAGE_SIZE),
                            lambda b, k, pg_ref, idx_ref, ok_ref: (pg_ref[(b * N_KV + g) * N_SEL + k * kstep + kk], 0, g, 0, 0))

    b4 = lambda b, k, *_: (b, 0, 0, 0)
    b5 = lambda b, k, *_: (b, 0, 0, 0, 0)
    grid_spec = pltpu.PrefetchScalarGridSpec(
        num_scalar_prefetch=3,
        grid=(DB, N_SEL // kstep),
        in_specs=[page_spec(g, kk) for g in range(N_KV) for kk in range(kstep)] + [
            pl.BlockSpec((1, N_KV, JPAD, HEAD_DIM), b4),
            pl.BlockSpec((1, N_KV, JPAD, N_BRANCH), b4),
            pl.BlockSpec((1, 2, N_KV, HEAD_DIM, 1), b5),
            pl.BlockSpec((1, 2, N_KV, HEAD_DIM, n_win), b5),
            pl.BlockSpec((1, 2, N_KV, HEAD_DIM, 1), b5),
            pl.BlockSpec((1, N_KV, JPAD, HEAD_DIM), b4),
        ],
        out_specs=pl.BlockSpec((1, N_KV, JPAD, HEAD_DIM), b4),
        scratch_shapes=[pltpu.VMEM((N_KV, JPAD, 1), F32), pltpu.VMEM((N_KV, JPAD, 1), F32),
                        pltpu.VMEM((N_KV, JPAD, HEAD_DIM), F32)],
    )
    return pl.pallas_call(
        functools.partial(_sample_sel_kernel, qpos=qpos, nb_past=nb_past, kstep=kstep),
        grid_spec=grid_spec,
        out_shape=jax.ShapeDtypeStruct((DB, N_KV, JPAD, HEAD_DIM), F32),
        compiler_params=_cparams(("arbitrary", "arbitrary")),
        name="sample_sel_win",
    )(pages, idx, ok, *([cache_t] * (N_KV * kstep)), q4, gate4, sel_new_t, win_t, win_new_t, o_c)


def _imp_matrix(n_cmp_rows, n_blk_cols):
    m = np.zeros((n_cmp_rows, n_blk_cols), np.float32)
    for j in range(n_blk_cols):
        for a in range(SEL_RATIO):
            for c in range(CMP_PARTS):
                i = SEL_RATIO * j + a - c
                if 0 <= i < n_cmp_rows:
                    m[i, j] += 1.0
    return m


def _rope_angles(pos):
    inv = ROPE_THETA ** (-jnp.arange(0, ROT_DIM, 2, dtype=F32) / ROT_DIM)
    ang = pos.astype(F32)[:, None] * inv[None, :]
    return jnp.cos(ang), jnp.sin(ang)


def _rope_tables(pos, period):
    cos, sin = _rope_angles(pos)
    n = pos.shape[0]
    rest1, rest0, zh = jnp.ones((n, period - ROT_DIM), F32), jnp.zeros((n, period - ROT_DIM), F32), jnp.zeros_like(sin)
    c = jnp.concatenate([cos, cos, rest1], axis=1)
    sa = jnp.concatenate([-sin, zh, rest0], axis=1)
    sb = jnp.concatenate([zh, sin, rest0], axis=1)
    return tuple(jnp.tile(t, (1, LANE // period)) for t in (c, sa, sb))


def _pad_heads(w, n_heads):
    w = w.reshape(w.shape[0], n_heads, HEAD_DIM)
    return jnp.pad(w, ((0, 0), (0, 0), (0, HEAD_SLOT - HEAD_DIM))).reshape(w.shape[0], n_heads * HEAD_SLOT)


def _to_rows(kv_t):
    return jnp.transpose(kv_t, (0, 4, 1, 2, 3))


def _to_cols(kv):
    return jnp.transpose(kv, (0, 2, 3, 4, 1))


def kernel(x_prompt, x_sample, state_pool, cache_cmp_kv, cache_sel_kv, state_win_kv, page_table,
           norm_mix, norm_ffn, pool_w, pool_scale, w_qg, w_o, norm_kv, w_kv, cmp_pe, cmp_w1, cmp_w2,
           mlp_up, mlp_down, norm_final):
    B, T, D = x_prompt.shape
    DB, S, _ = x_sample.shape
    past_len = page_table.shape[1] * PAGE_SIZE
    n_blk_p = T // L_SEL
    assert S == 1 and D == D_MODEL and T % PAGE_SIZE == 0 and n_blk_p <= HEAD_SLOT - HEAD_DIM and n_blk_p % SUBLANE == 0

    n_q = N_HEADS * HEAD_DIM
    w_q = w_qg[0][:, :n_q]
    w_gate = jnp.pad(w_qg[0][:, n_q:], ((0, 0), (0, GATE_LANES - N_HEADS * N_BRANCH)))
    w_kv5 = w_kv.reshape(D, N_BRANCH, 2, N_KV * HEAD_DIM)
    w_kpad = jnp.concatenate([_pad_heads(w_kv5[:, 1, 0], N_KV), _pad_heads(w_kv5[:, 2, 0], N_KV)], axis=1).astype(BF16)
    w_kv_b = w_kv.astype(BF16)
    w_o_b = w_o[0].astype(BF16)
    pool_w_b = pool_w[0].astype(BF16)
    up_b, down_b = mlp_up.astype(BF16), mlp_down.astype(BF16)
    g_mix0, g_mix1 = norm_mix[0:1], norm_mix[1:2]
    g_kv, g_fin = norm_kv[None, :], norm_final[None, :]
    w1_6 = cmp_w1.reshape(2, CMP_PARTS, STRIDE // 2, 2, HEAD_DIM, CMP_HIDDEN)
    w1pair = jnp.einsum('ab,eprshk->ersahbpk', jnp.eye(2, dtype=F32), w1_6)
    w1pair = w1pair.reshape(2, STRIDE // 2, 2 * 2 * HEAD_DIM, 2 * CMP_PARTS * CMP_HIDDEN).astype(BF16)
    w1_flat = cmp_w1.reshape(2, L_CMP * HEAD_DIM, CMP_HIDDEN).astype(BF16)
    pe8 = jnp.broadcast_to(cmp_pe.transpose(1, 0, 2).reshape(2, 1, L_CMP * HEAD_DIM), (2, SUBLANE, L_CMP * HEAD_DIM)).astype(BF16)
    w2pad = jnp.pad(cmp_w2[0], ((0, 0), (0, HEAD_SLOT - HEAD_DIM))).astype(BF16)
    w2t = cmp_w2.transpose(0, 2, 1).astype(BF16)

    h1, pool_tail = _pool_prompt(x_prompt, g_mix0, pool_w_b, pool_scale)
    pool_prompt = pool_tail[None, :, POOL_HALO - POOL_BUF:]
    h2 = _mlp(h1.reshape(B * T, D), norm_ffn[0:1], up_b[0], down_b[0], g_fin)
    pos_p = jnp.arange(T)
    cos_p, sin_p = _rope_angles(pos_p)
    cmp_t, sel_t, win_t, ksel, kwin, vsel, vwin, q_pad, gate_t = _proj_prompt(
        h2.reshape(B, T, D), g_kv, g_mix1, w_kv_b.T, w_kpad, _pad_heads(w_q, N_HEADS).astype(BF16), w_gate.T.astype(BF16),
        _rope_tables(pos_p, HEAD_SLOT), (cos_p.T, sin_p.T))
    kc_nat_p, _, vc_t_p = _cmp_finish(_cmp_parts(cmp_t, w1pair), pe8, w1_flat, w2pad, w2t)
    imp_t = jnp.asarray(_imp_matrix(kc_nat_p.shape[2], n_blk_p).T, BF16)
    o_p = _attn_prompt(q_pad, gate_t, kc_nat_p, vc_t_p, ksel, vsel, kwin, vwin, imp_t)
    y_p = _mlp(h2, norm_ffn[1:2], up_b[1], down_b[1], g_fin, o=o_p.reshape(B * T, D), w_o=w_o_b, final_norm=True)

    hist = jnp.concatenate([jnp.zeros((1, DB, D), F32), state_pool[0].transpose(1, 0, 2)], axis=0)
    hs1, us = _pool_sample(x_sample[:, 0], hist, g_mix0, pool_w_b, pool_scale)
    pool_sample = jnp.concatenate([state_pool[0][:, 1:], us[:, None]], axis=1)[None]
    hs2 = _mlp(hs1, norm_ffn[0:1], up_b[0], down_b[0], g_fin)
    cmp_s, sel_s, win_s, q_s, gate_s = _proj_sample(
        hs2, g_kv, g_mix1, w_kv_b, w_q.astype(BF16), w_gate.astype(BF16),
        _rope_tables(jnp.full((DB,), past_len, jnp.int32), HEAD_DIM))
    parts_s = _cmp_parts(_to_cols(cache_cmp_kv), w1pair, pages=page_table.reshape(-1), n_b=DB)
    _, kc_t_s, vc_t_s = _cmp_finish(parts_s, pe8, w1_flat, w2pad, w2t)
    n_valid = past_len // STRIDE - CMP_PARTS + 1
    nb_past = past_len // L_SEL
    n_blk_lanes = -(-(nb_past + 1) // LANE) * LANE
    jpad = ((0, 0), (0, 0), (0, JPAD - HPG), (0, 0))
    q4 = jnp.pad(q_s.astype(F32).reshape(DB, N_KV, HPG, HEAD_DIM), jpad)
    gate4 = jnp.pad(gate_s[:, :N_HEADS * N_BRANCH].reshape(DB, N_KV, HPG, N_BRANCH), jpad)
    imp_s = jnp.asarray(_imp_matrix(kc_t_s.shape[3], n_blk_lanes), BF16)
    o_c, idx_pad, val_pad = _sample_cmp(q4, kc_t_s, vc_t_s, imp_s, n_valid, nb_past)
    idx = idx_pad[:, :, :N_SEL]
    ok = (val_pad[:, :, :N_SEL] >= 0).astype(jnp.int32)
    bpp = PAGE_SIZE // L_SEL
    logical_page = (jnp.minimum(idx, nb_past - 1) // bpp).reshape(DB, -1)
    pages = jnp.take_along_axis(page_table, logical_page, axis=1)
    kv5 = lambda a, n, t: a.reshape(n, t, 2, N_KV, HEAD_DIM)
    o_s = _sample_sel(pages.reshape(-1), idx.reshape(-1), ok.reshape(-1), _to_cols(cache_sel_kv), q4, gate4,
                      _to_cols(kv5(sel_s, DB, 1)), _to_cols(state_win_kv), _to_cols(kv5(win_s, DB, 1)), o_c,
                      past_len, nb_past)
    o_s = o_s[:, :, :HPG].reshape(DB, D).astype(BF16)
    y_s = _mlp(hs2, norm_ffn[1:2], up_b[1], down_b[1], g_fin, o=o_s, w_o=w_o_b, final_norm=True)

    win_sample = jnp.concatenate([state_win_kv, kv5(win_s, DB, 1)], axis=1)[:, S:]
    return (y_p.reshape(B, T, D), y_s.reshape(DB, S, D),
            _to_rows(cmp_t), _to_rows(sel_t), _to_rows(win_t[..., T - min(WINDOW, T):]), pool_prompt,
            kv5(cmp_s, DB, 1), kv5(sel_s, DB, 1), win_sample, pool_sample)
```

```python
import functools
import math

import numpy as np
import jax
import jax.numpy as jnp
from jax import lax
from jax.experimental import pallas as pl
from jax.experimental.pallas import tpu as pltpu

D_MODEL = 1024
POOL_WINDOWS = (2, 4, 8, 16)
POOL_GROUP = D_MODEL // len(POOL_WINDOWS)
POOL_BUF = max(POOL_WINDOWS) - 1
POOL_HALO = 16
N_HEADS = 16
N_KV = 4
HPG = N_HEADS // N_KV
HEAD_DIM = 64
KV_LANES = N_KV * HEAD_DIM
ROW_LANES = 2 * KV_LANES
ROT_DIM = HEAD_DIM // 4
ROT_HALF = ROT_DIM // 2
ROPE_THETA = 500000.0
L_CMP = 32
STRIDE = 16
CMP_HIDDEN = 2 * HEAD_DIM
L_SEL = 64
SEL_RATIO = L_SEL // STRIDE
CMP_PARTS = L_CMP // STRIDE
N_SEL = 16
WINDOW = 512
N_BRANCH = 3
D_FF = 4 * D_MODEL
RMS_EPS = 1e-6
NEG_INF = -1e30
FORCE_SCORE = 1e9
Q_SCALE_LOG2 = HEAD_DIM ** -0.5 * math.log2(math.e)
PAGE_SIZE = 128
SEG_PER_PAGE = PAGE_SIZE // STRIDE
GATE_LANES = 128
LANE = 128
SUBLANE = 8
HEAD_SLOT = LANE
JPAD = SUBLANE
ONES_ROWS = 2 * SUBLANE

VMEM_LIMIT = 56 * 1024 * 1024

F32 = jnp.float32
BF16 = jnp.bfloat16


def _cparams(sem):
    return pltpu.CompilerParams(dimension_semantics=sem, vmem_limit_bytes=VMEM_LIMIT)


def _dot(a, b):
    return jnp.dot(a, b, preferred_element_type=F32)


def _dot_nt(a, b):
    return lax.dot_general(a, b, (((1,), (1,)), ((), ())), preferred_element_type=F32)


def _split3(x):
    hi = x.astype(BF16)
    r1 = x - hi.astype(F32)
    mid = r1.astype(BF16)
    lo = (r1 - mid.astype(F32)).astype(BF16)
    return hi, mid, lo


def _dot_exact_lhs(x, m_bf16):
    hi, mid, lo = _split3(x)
    return _dot(hi, m_bf16) + _dot(mid, m_bf16) + _dot(lo, m_bf16)


def _dot_exact_rhs(m_bf16, x):
    hi, mid, lo = _split3(x)
    return _dot(m_bf16, hi) + _dot(m_bf16, mid) + _dot(m_bf16, lo)


def _rms_scale(x):
    return x * lax.rsqrt(jnp.mean(x * x, axis=-1, keepdims=True) + RMS_EPS)


def _rope_lanes(x, c, sa, sb):
    outs = []
    for t in range(x.shape[1] // LANE):
        xc = x[:, t * LANE:(t + 1) * LANE]
        outs.append(xc * c + pltpu.roll(xc, LANE - ROT_HALF, axis=1) * sa + pltpu.roll(xc, ROT_HALF, axis=1) * sb)
    return jnp.concatenate(outs, axis=1)


def _masked_softmax(s, mask, axis, base2=False):
    sm = jnp.where(mask, s, NEG_INF)
    m = jnp.max(sm, axis=axis, keepdims=True)
    e = jnp.where(mask, (jnp.exp2 if base2 else jnp.exp)(sm - m), 0.0)
    l = jnp.sum(e, axis=axis, keepdims=True)
    return e * (1.0 / jnp.where(l > 0.0, l, 1.0))


def _pool_prompt_kernel(x_ref, xprev_ref, g_ref, w_ref, scale_ref, h_ref, buf_ref, *, tt):
    i = pl.program_id(1)
    x = x_ref[0]
    g = g_ref[...]
    u = _rms_scale(x) * g
    up = _rms_scale(xprev_ref[0]) * g
    up = jnp.where(i > 0, up, 0.0)
    ext = jnp.concatenate([up, u], axis=0)
    pos = i * tt + lax.broadcasted_iota(jnp.int32, (tt, 1), 0)
    scale = scale_ref[...]
    for gi, w in enumerate(POOL_WINDOWS):
        sl = slice(gi * POOL_GROUP, (gi + 1) * POOL_GROUP)
        s = ext[:, sl]
        k = 1
        while k < w:
            s = s + pltpu.roll(s, k, axis=0)
            k *= 2
        s = s[POOL_HALO:]
        cnt = jnp.minimum(pos + 1, w).astype(F32)
        d = s / cnt - u[:, sl]
        z = _dot(d.astype(BF16), w_ref[gi])
        h_ref[0, :, sl] = x[:, sl] + z * scale[:, sl]
    buf_ref[0] = u[tt - POOL_HALO:]


def _pool_prompt(x, g_mix, w_pool_bf16, scale, tt=512):
    B, T, D = x.shape
    hb = tt // POOL_HALO
    return pl.pallas_call(
        functools.partial(_pool_prompt_kernel, tt=tt),
        grid=(B, T // tt),
        in_specs=[
            pl.BlockSpec((1, tt, D), lambda b, i: (b, i, 0)),
            pl.BlockSpec((1, POOL_HALO, D), lambda b, i: (b, jnp.maximum(i * hb - 1, 0), 0)),
            pl.BlockSpec((1, D), lambda b, i: (0, 0)),
            pl.BlockSpec((len(POOL_WINDOWS), POOL_GROUP, POOL_GROUP), lambda b, i: (0, 0, 0)),
            pl.BlockSpec((1, D), lambda b, i: (0, 0)),
        ],
        out_specs=[
            pl.BlockSpec((1, tt, D), lambda b, i: (b, i, 0)),
            pl.BlockSpec((1, POOL_HALO, D), lambda b, i: (b, 0, 0)),
        ],
        out_shape=[jax.ShapeDtypeStruct((B, T, D), F32), jax.ShapeDtypeStruct((B, POOL_HALO, D), F32)],
        compiler_params=_cparams(("arbitrary", "arbitrary")),
        name="pool_prompt",
    )(x, x, g_mix, w_pool_bf16, scale)


def _pool_sample_kernel(x_ref, hist_ref, g_ref, w_ref, scale_ref, h_ref, u_ref):
    x = x_ref[...]
    u = _rms_scale(x) * g_ref[...]
    u_ref[...] = u
    scale = scale_ref[...]
    for gi, w in enumerate(POOL_WINDOWS):
        sl = slice(gi * POOL_GROUP, (gi + 1) * POOL_GROUP)
        s = u[:, sl]
        for r in range(1, w):
            s = s + hist_ref[POOL_BUF + 1 - r][:, sl]
        d = s / float(w) - u[:, sl]
        z = _dot(d.astype(BF16), w_ref[gi])
        h_ref[:, sl] = x[:, sl] + z * scale[:, sl]


def _pool_sample(x, hist, g_mix, w_pool_bf16, scale):
    DB, D = x.shape
    return pl.pallas_call(
        _pool_sample_kernel,
        out_shape=[jax.ShapeDtypeStruct((DB, D), F32), jax.ShapeDtypeStruct((DB, D), F32)],
        compiler_params=pltpu.CompilerParams(vmem_limit_bytes=VMEM_LIMIT),
        name="pool_sample",
    )(x, hist, g_mix, w_pool_bf16, scale)


def _mlp_kernel(*refs, has_attn, final_norm):
    if has_attn:
        h_ref, o_ref, wo_ref, g_ref, wup_ref, wdn_ref, gfin_ref, out_ref, hres, xn, acc = refs
    else:
        h_ref, g_ref, wup_ref, wdn_ref, gfin_ref, out_ref, hres, xn, acc = refs
    f = pl.program_id(1)

    @pl.when(f == 0)
    def _():
        h = h_ref[...]
        if has_attn:
            h = h + _dot(o_ref[...], wo_ref[...])
        hres[...] = h
        xn[...] = (_rms_scale(h) * g_ref[...]).astype(BF16)
        acc[...] = jnp.zeros_like(acc)

    a = jnp.maximum(_dot(xn[...], wup_ref[...]), 0.0)
    acc[...] += _dot((a * a).astype(BF16), wdn_ref[...])

    @pl.when(f == pl.num_programs(1) - 1)
    def _():
        y = hres[...] + acc[...]
        if final_norm:
            y = _rms_scale(y) * gfin_ref[...]
        out_ref[...] = y


def _mlp(h, g_ffn, w_up, w_down, g_final, o=None, w_o=None, final_norm=False, tm=1024, tf=1024):
    M, D = h.shape
    tm = min(tm, M)
    has_attn = o is not None
    row = lambda i, f: (i, 0)
    const = lambda i, f: (0, 0)
    in_specs = [pl.BlockSpec((tm, D), row)]
    args = [h]
    if has_attn:
        in_specs += [pl.BlockSpec((tm, D), row), pl.BlockSpec((D, D), const)]
        args += [o, w_o]
    in_specs += [
        pl.BlockSpec((1, D), const),
        pl.BlockSpec((D, tf), lambda i, f: (0, f)),
        pl.BlockSpec((tf, D), lambda i, f: (f, 0)),
        pl.BlockSpec((1, D), const),
    ]
    args += [g_ffn, w_up, w_down, g_final]
    return pl.pallas_call(
        functools.partial(_mlp_kernel, has_attn=has_attn, final_norm=final_norm),
        grid=(M // tm, D_FF // tf),
        in_specs=in_specs,
        out_specs=pl.BlockSpec((tm, D), row),
        out_shape=jax.ShapeDtypeStruct((M, D), F32),
        scratch_shapes=[pltpu.VMEM((tm, D), F32), pltpu.VMEM((tm, D), BF16), pltpu.VMEM((tm, D), F32)],
        compiler_params=_cparams(("arbitrary", "arbitrary")),
        name="mlp",
    )(*args)


def _proj_prompt_kernel(h_ref, gkv_ref, gq_ref, wkvt_ref, wkp_ref, wqp_ref, wgt_ref, c_ref, sa_ref, sb_ref, ct_ref, st_ref,
                        cmp_ref, sel_ref, win_ref, ksel_ref, kwin_ref, vsel_ref, vwin_ref, q_ref, gate_ref, *, tt):
    i = pl.program_id(1)
    y = _rms_scale(h_ref[0])
    xkv = (y * gkv_ref[...]).astype(BF16)
    xq = (y * gq_ref[...]).astype(BF16)

    kvt = _dot_nt(wkvt_ref[...], xkv)
    cos_t, sin_t = ct_ref[...], st_ref[...]
    for br, (oref, vref) in enumerate(((cmp_ref, None), (sel_ref, vsel_ref), (win_ref, vwin_ref))):
        base = br * ROW_LANES
        for g in range(N_KV):
            r0 = base + g * HEAD_DIM
            x1, x2 = kvt[r0:r0 + ROT_HALF], kvt[r0 + ROT_HALF:r0 + ROT_DIM]
            oref[0, 0, g] = jnp.concatenate(
                [x1 * cos_t - x2 * sin_t, x2 * cos_t + x1 * sin_t, kvt[r0 + ROT_DIM:r0 + HEAD_DIM]], axis=0)
            v = kvt[r0 + KV_LANES:r0 + KV_LANES + HEAD_DIM]
            oref[0, 1, g] = v
            if vref is not None:
                for c in range(tt // LANE):
                    vref[0, g, c] = v[:, c * LANE:(c + 1) * LANE].astype(BF16)

    c, sa, sb = c_ref[...], sa_ref[...], sb_ref[...]
    kp = _rope_lanes(_dot(xkv, wkp_ref[...]), c, sa, sb)
    pos = i * tt + lax.broadcasted_iota(jnp.int32, (tt, 1), 0)
    half_lane = lax.broadcasted_iota(jnp.int32, (1, HEAD_SLOT - HEAD_DIM), 1)
    blk_onehot = jnp.where(half_lane == pos // L_SEL, 1.0, 0.0)
    no_bias = jnp.zeros((tt, HEAD_SLOT - HEAD_DIM), F32)

    def head(x, h):
        return x[:, h * HEAD_DIM:(h + 1) * HEAD_DIM]

    for g in range(N_KV):
        ksel_ref[0, g] = jnp.concatenate([head(kp, g), blk_onehot], axis=1).astype(BF16)
        kwin_ref[0, g] = jnp.concatenate([head(kp, N_KV + g), no_bias], axis=1).astype(BF16)

    q = _rope_lanes(_dot(xq, wqp_ref[...]), c, sa, sb) * Q_SCALE_LOG2
    q_ref[0] = jnp.concatenate([piece for h in range(N_HEADS) for piece in (head(q, h), no_bias)], axis=1).astype(BF16)
    gate_ref[0] = jax.nn.sigmoid(_dot_nt(wgt_ref[...], xq))


def _proj_prompt(h, g_kv, g_q, w_kvt, w_k, w_q, w_gt, rope_nat, rope_t, tt=512):
    B, T, D = h.shape
    const2 = lambda b, i: (0, 0)
    kv_t = jax.ShapeDtypeStruct((B, 2, N_KV, HEAD_DIM, T), F32)
    k_nat = jax.ShapeDtypeStruct((B, N_KV, T, HEAD_SLOT), BF16)
    v_t = jax.ShapeDtypeStruct((B, N_KV, T // LANE, HEAD_DIM, LANE), BF16)
    kv_spec = pl.BlockSpec((1, 2, N_KV, HEAD_DIM, tt), lambda b, i: (b, 0, 0, 0, i))
    k_spec = pl.BlockSpec((1, N_KV, tt, HEAD_SLOT), lambda b, i: (b, 0, i, 0))
    v_spec = pl.BlockSpec((1, N_KV, tt // LANE, HEAD_DIM, LANE), lambda b, i: (b, 0, i, 0, 0))
    return pl.pallas_call(
        functools.partial(_proj_prompt_kernel, tt=tt),
        grid=(B, T // tt),
        in_specs=[
            pl.BlockSpec((1, tt, D), lambda b, i: (b, i, 0)),
            pl.BlockSpec((1, D), const2), pl.BlockSpec((1, D), const2),
            pl.BlockSpec(w_kvt.shape, const2), pl.BlockSpec(w_k.shape, const2),
            pl.BlockSpec(w_q.shape, const2), pl.BlockSpec(w_gt.shape, const2),
            pl.BlockSpec((tt, LANE), lambda b, i: (i, 0)), pl.BlockSpec((tt, LANE), lambda b, i: (i, 0)),
            pl.BlockSpec((tt, LANE), lambda b, i: (i, 0)),
            pl.BlockSpec((ROT_HALF, tt), lambda b, i: (0, i)), pl.BlockSpec((ROT_HALF, tt), lambda b, i: (0, i)),
        ],
        out_specs=[kv_spec, kv_spec, kv_spec, k_spec, k_spec, v_spec, v_spec,
                   pl.BlockSpec((1, tt, N_HEADS * HEAD_SLOT), lambda b, i: (b, i, 0)),
                   pl.BlockSpec((1, GATE_LANES, tt), lambda b, i: (b, 0, i))],
        out_shape=[kv_t, kv_t, kv_t, k_nat, k_nat, v_t, v_t,
                   jax.ShapeDtypeStruct((B, T, N_HEADS * HEAD_SLOT), BF16),
                   jax.ShapeDtypeStruct((B, GATE_LANES, T), F32)],
        compiler_params=_cparams(("arbitrary", "arbitrary")),
        name="kv_q_proj_prompt",
    )(h, g_kv, g_q, w_kvt, w_k, w_q, w_gt, *rope_nat, *rope_t)


def _proj_sample_kernel(h_ref, gkv_ref, gq_ref, wkv_ref, wq_ref, wg_ref, c_ref, sa_ref, sb_ref,
                        cmp_ref, sel_ref, win_ref, q_ref, gate_ref):
    y = _rms_scale(h_ref[...])
    xkv = (y * gkv_ref[...]).astype(BF16)
    xq = (y * gq_ref[...]).astype(BF16)
    c, sa, sb = c_ref[...], sa_ref[...], sb_ref[...]
    kv = _dot(xkv, wkv_ref[...])
    for br, oref in enumerate((cmp_ref, sel_ref, win_ref)):
        oref[:, 0:KV_LANES] = _rope_lanes(kv[:, br * ROW_LANES:br * ROW_LANES + KV_LANES], c, sa, sb)
        oref[:, KV_LANES:ROW_LANES] = kv[:, br * ROW_LANES + KV_LANES:(br + 1) * ROW_LANES]
    q_ref[...] = (_rope_lanes(_dot(xq, wq_ref[...]), c, sa, sb) * (HEAD_DIM ** -0.5)).astype(BF16)
    gate_ref[...] = jax.nn.sigmoid(_dot(xq, wg_ref[...]))


def _proj_sample(h, g_kv, g_q, w_kv, w_q, w_g, rope_nat):
    M, D = h.shape
    return pl.pallas_call(
        _proj_sample_kernel,
        out_shape=[jax.ShapeDtypeStruct((M, ROW_LANES), F32)] * 3
                  + [jax.ShapeDtypeStruct((M, D), BF16), jax.ShapeDtypeStruct((M, GATE_LANES), F32)],
        compiler_params=pltpu.CompilerParams(vmem_limit_bytes=VMEM_LIMIT),
        name="kv_q_proj_sample",
    )(h, g_kv, g_q, w_kv, w_q, w_g, *rope_nat)


def _parts_kernel(*refs, pg, prefetch):
    refs = refs[prefetch:]
    x_refs, w_ref, out_ref, xs = refs[:pg], refs[pg], refs[pg + 1], refs[pg + 2]
    n_rows = pg * SEG_PER_PAGE
    stages = [(e, gp) for e in range(2) for gp in range(N_KV // 2)]

    def fill(k):
        e, gp = stages[k]
        for p, xr in enumerate(x_refs):
            xt = jnp.concatenate([xr[0, e, 2 * gp], xr[0, e, 2 * gp + 1]], axis=0)
            xs[k % 2, p * PAGE_SIZE:(p + 1) * PAGE_SIZE, :] = xt.T

    fill(0)
    for k, (e, gp) in enumerate(stages):
        if k + 1 < len(stages):
            fill(k + 1)
        acc = jnp.zeros((n_rows, 2 * 2 * CMP_HIDDEN), F32)
        for rp in range(STRIDE // 2):
            a = xs.at[k % 2][pl.ds(2 * rp, n_rows, stride=STRIDE), :]
            b = xs.at[k % 2][pl.ds(2 * rp + 1, n_rows, stride=STRIDE), :]
            acc = acc + _dot(jnp.concatenate([a, b], axis=1).astype(BF16), w_ref[e, rp])
        c0 = (e * N_KV + 2 * gp) * 2 * CMP_HIDDEN
        out_ref[0, :, c0:c0 + 2 * 2 * CMP_HIDDEN] = acc


def _cmp_parts(kv_t, w1pair, pages=None, n_b=None, pg=16):
    blk = (1, 2, N_KV, HEAD_DIM, PAGE_SIZE)
    width = 2 * N_KV * 2 * CMP_HIDDEN
    if pages is None:
        n_b, ppb = kv_t.shape[0], kv_t.shape[-1] // PAGE_SIZE
        pg = min(pg, ppb)
        in_specs = [pl.BlockSpec(blk, functools.partial(lambda b, s, p: (b, 0, 0, 0, s * pg + p), p=p)) for p in range(pg)]
        in_specs.append(pl.BlockSpec(w1pair.shape, lambda b, s: (0, 0, 0, 0)))
        out_spec = pl.BlockSpec((1, pg * SEG_PER_PAGE, width), lambda b, s: (b, s, 0))
        grid_kw = dict(grid=(n_b, ppb // pg), in_specs=in_specs, out_specs=out_spec,
                       scratch_shapes=[pltpu.VMEM((2, pg * PAGE_SIZE, LANE), F32)])
        args = [kv_t] * pg + [w1pair]
        prefetch = 0
    else:
        ppb = pages.shape[0] // n_b
        pg = min(pg, ppb)
        in_specs = [pl.BlockSpec(blk, functools.partial(lambda b, s, pr, p: (pr[b * ppb + s * pg + p], 0, 0, 0, 0), p=p))
                    for p in range(pg)]
        in_specs.append(pl.BlockSpec(w1pair.shape, lambda b, s, pr: (0, 0, 0, 0)))
        out_spec = pl.BlockSpec((1, pg * SEG_PER_PAGE, width), lambda b, s, pr: (b, s, 0))
        grid_kw = dict(grid_spec=pltpu.PrefetchScalarGridSpec(
            num_scalar_prefetch=1, grid=(n_b, ppb // pg), in_specs=in_specs, out_specs=out_spec,
            scratch_shapes=[pltpu.VMEM((2, pg * PAGE_SIZE, LANE), F32)]))
        args = [pages] + [kv_t] * pg + [w1pair]
        prefetch = 1
    return pl.pallas_call(
        functools.partial(_parts_kernel, pg=pg, prefetch=prefetch),
        out_shape=jax.ShapeDtypeStruct((n_b, ppb * SEG_PER_PAGE, width), F32),
        compiler_params=_cparams(("arbitrary", "arbitrary")),
        name="cmp_parts",
        **grid_kw,
    )(*args)


def _cmp_finish_kernel(parts_ref, pe_ref, w1f_ref, w2p_ref, w2t_ref, knat_ref, kt_ref, vt_ref, *, n_seg):
    rows = lax.broadcasted_iota(jnp.int32, (n_seg, 1), 0)
    for e in range(2):
        pe_sum = _dot(pe_ref[e], w1f_ref[e])[0:1]
        for g in range(N_KV):
            c0 = (e * N_KV + g) * 2 * CMP_HIDDEN
            first = parts_ref[0, :, c0:c0 + CMP_HIDDEN]
            second = pltpu.roll(parts_ref[0, :, c0 + CMP_HIDDEN:c0 + 2 * CMP_HIDDEN], n_seg - 1, axis=0)
            act = jax.nn.gelu(first + second + pe_sum)
            act = jnp.where(rows < n_seg - 1, act, 0.0).astype(BF16)
            o_t = _dot_nt(w2t_ref[e], act).astype(BF16)
            if e == 0:
                knat_ref[0, g] = _dot(act, w2p_ref[...]).astype(BF16)
                kt_ref[0, g] = o_t
            else:
                vt_ref[0, g] = o_t


def _cmp_finish(parts, pe8, w1f, w2pad, w2t):
    n_b, n_seg, width = parts.shape
    c3 = lambda b: (0, 0, 0)
    return pl.pallas_call(
        functools.partial(_cmp_finish_kernel, n_seg=n_seg),
        grid=(n_b,),
        in_specs=[
            pl.BlockSpec((1, n_seg, width), lambda b: (b, 0, 0)),
            pl.BlockSpec(pe8.shape, c3), pl.BlockSpec(w1f.shape, c3),
            pl.BlockSpec(w2pad.shape, lambda b: (0, 0)), pl.BlockSpec(w2t.shape, c3),
        ],
        out_specs=[pl.BlockSpec((1, N_KV, n_seg, HEAD_SLOT), lambda b: (b, 0, 0, 0)),
                   pl.BlockSpec((1, N_KV, HEAD_DIM, n_seg), lambda b: (b, 0, 0, 0)),
                   pl.BlockSpec((1, N_KV, HEAD_DIM, n_seg), lambda b: (b, 0, 0, 0))],
        out_shape=[jax.ShapeDtypeStruct((n_b, N_KV, n_seg, HEAD_SLOT), BF16),
                   jax.ShapeDtypeStruct((n_b, N_KV, HEAD_DIM, n_seg), BF16),
                   jax.ShapeDtypeStruct((n_b, N_KV, HEAD_DIM, n_seg), BF16)],
        compiler_params=_cparams(("arbitrary",)),
        name="cmp_finish",
    )(parts, pe8, w1f, w2pad, w2t)


def _topk_rows(score, n_sel, n_live):
    n_blk = score.shape[0]
    n_tiles = n_blk // SUBLANE
    tiles = [score[r * SUBLANE:(r + 1) * SUBLANE] for r in range(n_tiles)]
    sub = lax.broadcasted_iota(jnp.int32, (SUBLANE, 1), 0)

    def count_tile(rank, ri):
        rank = list(rank)
        for i in range(ri * SUBLANE, (ri + 1) * SUBLANE):
            bi = jnp.broadcast_to(score[i:i + 1], tiles[0].shape)
            for r in range(n_tiles):
                if r > ri:
                    ahead = jnp.where(bi >= tiles[r], 1.0, 0.0)
                elif r < ri:
                    ahead = jnp.where(bi > tiles[r], 1.0, 0.0)
                else:
                    tie = jnp.where(sub > i % SUBLANE, 1.0, 0.0)
                    ahead = jnp.where(bi > tiles[r], 1.0, jnp.where(bi == tiles[r], tie, 0.0))
                rank[r] = rank[r] + ahead
        return tuple(rank)

    rank = tuple(jnp.zeros(tiles[0].shape, F32) for _ in range(n_tiles))
    for ri in range(n_tiles):
        rank = lax.cond(ri * SUBLANE < n_live, functools.partial(count_tile, ri=ri), lambda r: r, rank)
    return jnp.concatenate(rank, axis=0) < float(n_sel)


def _attn_prompt_kernel(q_ref, gt_ref, kc_ref, vc_ref, ks_ref, vs_ref, kw_ref, vw_ref, impt_ref, eye_ref, o_ref,
                        *, tq, kc, t_len):
    i = pl.program_id(1)
    s0 = i * tq
    n_cmp = kc_ref.shape[2]
    n_blk = t_len // L_SEL
    rows = HPG * tq
    qpos = s0 + lax.broadcasted_iota(jnp.int32, (1, tq), 1)
    qpos4 = jnp.concatenate([qpos] * HPG, axis=1)
    blk = lax.broadcasted_iota(jnp.int32, (n_blk, 1), 0)
    cur = qpos // L_SEL
    valid = blk <= cur
    forced = (blk == 0) | (blk == cur) | (blk == cur - 1)
    c_end = lax.broadcasted_iota(jnp.int32, (n_cmp, 1), 0) * STRIDE + (L_CMP - 1)
    cmask = c_end <= qpos4
    slot_lane = lax.broadcasted_iota(jnp.int32, (1, HEAD_SLOT), 1)
    n_full = s0 // kc
    w_len = WINDOW + tq
    w0 = pl.multiple_of(jnp.clip(s0 - WINDOW, 0, t_len - w_len), LANE)
    wpos = w0 + lax.broadcasted_iota(jnp.int32, (w_len, 1), 0)
    wmask = (wpos <= qpos4) & (wpos >= qpos4 - WINDOW)
    gt = gt_ref[0]
    heads = []

    def with_ones(v):
        return jnp.concatenate([v, jnp.ones((ONES_ROWS, v.shape[1]), BF16)], axis=0)

    def flash_update(s, v, carry):
        m_i, acc = carry
        m_new = jnp.maximum(m_i, jnp.max(s, axis=0, keepdims=True))
        e = jnp.exp2(s - m_new)
        return m_new, jnp.exp2(m_i - m_new) * acc + _dot(v, e.astype(BF16))

    def normalise(acc):
        return acc[0:HEAD_DIM] * (1.0 / acc[HEAD_DIM:HEAD_DIM + 1])

    def gate_row(g, br):
        return jnp.concatenate([gt[(g * HPG + j) * N_BRANCH + br:(g * HPG + j) * N_BRANCH + br + 1] for j in range(HPG)], axis=1)

    for gp in range(N_KV // 2):
        gs = (2 * gp, 2 * gp + 1)
        qg = [jnp.concatenate([q_ref[0, :, (g * HPG + j) * HEAD_SLOT:(g * HPG + j + 1) * HEAD_SLOT] for j in range(HPG)], axis=0)
              for g in gs]

        s_c = [_dot_nt(kc_ref[0, g], qg[t]) for t, g in enumerate(gs)]
        p_c = [_masked_softmax(s, cmask, 0, base2=True) for s in s_c]
        o_c = [_dot(vc_ref[0, g], p_c[t].astype(BF16)) for t, g in enumerate(gs)]
        qa = []
        for t in range(2):
            p_sum = p_c[t][:, 0:tq]
            for j in range(1, HPG):
                p_sum = p_sum + p_c[t][:, j * tq:(j + 1) * tq]
            imp = _dot_exact_rhs(impt_ref[...], p_sum)
            score = jnp.where(forced, FORCE_SCORE, jnp.where(valid, imp, -1.0))
            sel = _topk_rows(score, min(N_SEL, n_blk), (s0 + tq - 1) // L_SEL + 1) & valid
            pieces = [jnp.ones((HEAD_DIM, tq), F32), jnp.where(sel, 1.0, 0.0)]
            if n_blk < HEAD_SLOT - HEAD_DIM:
                pieces.append(jnp.zeros((HEAD_SLOT - HEAD_DIM - n_blk, tq), F32))
            sel_pad = jnp.concatenate(pieces, axis=0)
            sel_q = _dot_nt(eye_ref[...], sel_pad.astype(BF16))
            bias = ((sel_q - 1.0) * -NEG_INF).astype(BF16)
            qa.append(jnp.where(slot_lane < HEAD_DIM, qg[t], jnp.concatenate([bias] * HPG, axis=0)))

        def scores(t, c):
            return _dot_nt(ks_ref[0, gs[t], pl.ds(pl.multiple_of(c * kc, kc), kc), :], qa[t])

        def values(t, c):
            return with_ones(jnp.concatenate([vs_ref[0, gs[t], c * (kc // LANE) + u] for u in range(kc // LANE)], axis=1))

        def chunk(c, carry):
            s = [scores(0, c), scores(1, c)]
            return tuple(flash_update(s[t], values(t, c), carry[t]) for t in range(2))

        init1 = (jnp.full((1, rows), NEG_INF, F32), jnp.zeros((HEAD_DIM + ONES_ROWS, rows), F32))
        carry = lax.fori_loop(0, n_full, chunk, (init1, init1))
        causal = (n_full * kc + lax.broadcasted_iota(jnp.int32, (kc, 1), 0)) <= qpos4
        s_d = [scores(0, n_full), scores(1, n_full)]
        last = [flash_update(jnp.where(causal, s_d[t], NEG_INF), values(t, n_full), carry[t]) for t in range(2)]
        o_s = [normalise(acc) for _, acc in last]

        s_w = [_dot_nt(kw_ref[0, g, pl.ds(w0, w_len), :], qg[t]) for t, g in enumerate(gs)]
        o_w = []
        for t, g in enumerate(gs):
            sm = jnp.where(wmask, s_w[t], NEG_INF)
            e = jnp.exp2(sm - jnp.max(sm, axis=0, keepdims=True))
            v_w = with_ones(jnp.concatenate([vw_ref[0, g, w0 // LANE + u] for u in range(w_len // LANE)], axis=1))
            o_w.append(normalise(_dot(v_w, e.astype(BF16))))

        for t, g in enumerate(gs):
            comb = gate_row(g, 0) * o_c[t] + gate_row(g, 1) * o_s[t] + gate_row(g, 2) * o_w[t]
            heads += [comb[:, j * tq:(j + 1) * tq] for j in range(HPG)]

    o_ref[0] = jnp.concatenate(heads, axis=0).T.astype(BF16)


def _attn_prompt(q_pad, gate_t, kc_nat, vc_t, ksel, vsel, kwin, vwin, imp_t, tq=256, kc=512):
    B, T, _ = q_pad.shape
    n_cmp = kc_nat.shape[2]
    eye = jnp.eye(tq, dtype=BF16)
    b4 = lambda b, i: (b, 0, 0, 0)
    b5 = lambda b, i: (b, 0, 0, 0, 0)
    return pl.pallas_call(
        functools.partial(_attn_prompt_kernel, tq=tq, kc=kc, t_len=T),
        grid=(B, T // tq),
        in_specs=[
            pl.BlockSpec((1, tq, N_HEADS * HEAD_SLOT), lambda b, i: (b, i, 0)),
            pl.BlockSpec((1, GATE_LANES, tq), lambda b, i: (b, 0, i)),
            pl.BlockSpec((1, N_KV, n_cmp, HEAD_SLOT), b4),
            pl.BlockSpec((1, N_KV, HEAD_DIM, n_cmp), b4),
            pl.BlockSpec((1, N_KV, T, HEAD_SLOT), b4),
            pl.BlockSpec((1, N_KV, T // LANE, HEAD_DIM, LANE), b5),
            pl.BlockSpec((1, N_KV, T, HEAD_SLOT), b4),
            pl.BlockSpec((1, N_KV, T // LANE, HEAD_DIM, LANE), b5),
            pl.BlockSpec(imp_t.shape, lambda b, i: (0, 0)),
            pl.BlockSpec(eye.shape, lambda b, i: (0, 0)),
        ],
        out_specs=pl.BlockSpec((1, tq, D_MODEL), lambda b, i: (b, i, 0)),
        out_shape=jax.ShapeDtypeStruct((B, T, D_MODEL), BF16),
        compiler_params=_cparams(("arbitrary", "arbitrary")),
        name="attn_prompt",
    )(q_pad, gate_t, kc_nat, vc_t, ksel, vsel, kwin, vwin, imp_t, eye)


def _sample_cmp_kernel(q_ref, kt_ref, vt_ref, imp_ref, oc_ref, idx_ref, val_ref, *, n_valid, cur, n_blk_lanes):
    nb, n_cmp = kt_ref.shape[0], kt_ref.shape[3]
    n_rows = nb * N_KV
    cmask = lax.broadcasted_iota(jnp.int32, (1, n_cmp), 1) < n_valid
    sums = []
    for bi in range(nb):
        for g in range(N_KV):
            p = _masked_softmax(_dot(q_ref[bi, g].astype(BF16), kt_ref[bi, g]), cmask, -1)
            oc_ref[bi, g] = _dot_nt(p.astype(BF16), vt_ref[bi, g])
            sums.append(jnp.sum(p[0:HPG], axis=0, keepdims=True))
    imp = _dot_exact_lhs(jnp.concatenate(sums, axis=0), imp_ref[...])
    blk = lax.broadcasted_iota(jnp.int32, (1, n_blk_lanes), 1)
    blk_f = blk.astype(F32)
    valid = blk <= cur
    forced = (blk == 0) | (blk == cur) | (blk == cur - 1)
    score = jnp.where(forced, FORCE_SCORE, jnp.where(valid, imp, -1.0))
    score = jnp.where(valid, score, -2.0)
    out_lane = lax.broadcasted_iota(jnp.int32, (1, LANE), 1)
    idx_acc = jnp.zeros((n_rows, LANE), F32)
    val_acc = jnp.full((n_rows, LANE), -1.0, F32)
    for t in range(N_SEL):
        mx = jnp.max(score, axis=-1, keepdims=True)
        ix = jnp.min(jnp.where(score == mx, blk_f, float(n_blk_lanes)), axis=-1, keepdims=True)
        idx_acc = jnp.where(out_lane == t, ix, idx_acc)
        val_acc = jnp.where(out_lane == t, mx, val_acc)
        score = jnp.where(blk_f == ix, -3.0, score)
    idx_ref[...] = idx_acc.astype(jnp.int32)
    val_ref[...] = val_acc


def _sample_cmp(q4, kc_t, vc_t, imp_mat, n_valid, cur, nb=8):
    DB = q4.shape[0]
    n_cmp = kc_t.shape[3]
    nb = nb if DB % nb == 0 else DB
    b4 = lambda b: (b, 0, 0, 0)
    return pl.pallas_call(
        functools.partial(_sample_cmp_kernel, n_valid=n_valid, cur=cur, n_blk_lanes=imp_mat.shape[1]),
        grid=(DB // nb,),
        in_specs=[
            pl.BlockSpec((nb, N_KV, JPAD, HEAD_DIM), b4),
            pl.BlockSpec((nb, N_KV, HEAD_DIM, n_cmp), b4),
            pl.BlockSpec((nb, N_KV, HEAD_DIM, n_cmp), b4),
            pl.BlockSpec(imp_mat.shape, lambda b: (0, 0)),
        ],
        out_specs=[
            pl.BlockSpec((nb, N_KV, JPAD, HEAD_DIM), b4),
            pl.BlockSpec((nb * N_KV, LANE), lambda b: (b, 0)),
            pl.BlockSpec((nb * N_KV, LANE), lambda b: (b, 0)),
        ],
        out_shape=[jax.ShapeDtypeStruct((DB, N_KV, JPAD, HEAD_DIM), F32),
                   jax.ShapeDtypeStruct((DB * N_KV, LANE), jnp.int32),
                   jax.ShapeDtypeStruct((DB * N_KV, LANE), F32)],
        compiler_params=_cparams(("arbitrary",)),
        name="sample_cmp_topk",
    )(q4, kc_t, vc_t, imp_mat)


def _sample_sel_kernel(pg_ref, idx_ref, ok_ref, *refs, qpos, nb_past, kstep):
    del pg_ref
    page_refs = refs[:N_KV * kstep]
    q_ref, gate_ref, selnew_ref, win_ref, winnew_ref, oc_ref, o_ref, m_sc, l_sc, acc_sc = refs[N_KV * kstep:]
    b = pl.program_id(0)
    k = pl.program_id(1)
    bpp = PAGE_SIZE // L_SEL

    @pl.when(k == 0)
    def _():
        m_sc[...] = jnp.full(m_sc.shape, NEG_INF, F32)
        l_sc[...] = jnp.zeros_like(l_sc)
        acc_sc[...] = jnp.zeros_like(acc_sc)

    lane = lax.broadcasted_iota(jnp.int32, (1, PAGE_SIZE), 1)
    for g in range(N_KV):
        k_parts, v_parts, keep_parts = [], [], []
        for kk in range(kstep):
            bref = page_refs[g * kstep + kk]
            n = (b * N_KV + g) * N_SEL + k * kstep + kk
            idx = idx_ref[n]
            tail = idx >= nb_past
            new_col = (lane + jnp.where(tail, 0, PAGE_SIZE)) == 0
            k_parts.append(jnp.where(new_col, selnew_ref[0, 0, g], bref[0, 0, 0]).astype(BF16))
            v_parts.append(jnp.where(new_col, selnew_ref[0, 1, g], bref[0, 1, 0]).astype(BF16))
            first_pos = jnp.where(tail, idx * L_SEL, (idx // bpp) * PAGE_SIZE)
            half = jnp.where(tail, 0, idx % bpp)
            kpos = first_pos + lane + jnp.where(ok_ref[n] > 0, 0, qpos + 1)
            keep_parts.append(jnp.where((lane // L_SEL == half) & (kpos <= qpos), 1.0, 0.0))
        keep = jnp.concatenate(keep_parts, axis=1) > 0.5
        v_t = jnp.concatenate(v_parts, axis=1)
        s = _dot(q_ref[0, g].astype(BF16), jnp.concatenate(k_parts, axis=1))
        sm = jnp.where(keep, s, NEG_INF)
        m_new = jnp.maximum(m_sc[g], jnp.max(sm, axis=-1, keepdims=True))
        alpha = jnp.exp(m_sc[g] - m_new)
        e = jnp.where(keep, jnp.exp(sm - m_new), 0.0)
        l_sc[g] = alpha * l_sc[g] + jnp.sum(e, axis=-1, keepdims=True)
        acc_sc[g] = alpha * acc_sc[g] + _dot_nt(e.astype(BF16), v_t)
        m_sc[g] = m_new

    @pl.when(k == pl.num_programs(1) - 1)
    def _():
        n_win = win_ref.shape[4]
        ext_lane = lax.broadcasted_iota(jnp.int32, (1, LANE), 1)
        wpos = qpos - n_win + lax.broadcasted_iota(jnp.int32, (1, n_win + LANE), 1)
        wmask = (wpos >= 0) & (wpos <= qpos) & (wpos >= qpos - WINDOW)
        for g in range(N_KV):
            k_t = jnp.concatenate([win_ref[0, 0, g], jnp.where(ext_lane == 0, winnew_ref[0, 0, g], 0.0)], axis=1)
            v_t = jnp.concatenate([win_ref[0, 1, g], jnp.where(ext_lane == 0, winnew_ref[0, 1, g], 0.0)], axis=1)
            p_w = _masked_softmax(_dot(q_ref[0, g].astype(BF16), k_t.astype(BF16)), wmask, -1)
            o_w = _dot_nt(p_w.astype(BF16), v_t.astype(BF16))
            gate = gate_ref[0, g]
            o_ref[0, g] = gate[:, 0:1] * oc_ref[0, g] + gate[:, 1:2] * (acc_sc[g] / l_sc[g]) + gate[:, 2:3] * o_w


def _sample_sel(pages, idx, ok, cache_t, q4, gate4, sel_new_t, win_t, win_new_t, o_c, qpos, nb_past, kstep=N_SEL):
    DB = q4.shape[0]
    n_win = win_t.shape[4]

    def page_spec(g, kk):
        return pl.BlockSpec((1, 2, 1, HEAD_DIM, PAGE_SIZE),
                            lambda b, k, pg_ref, idx_ref, ok_ref: (pg_ref[(b * N_KV + g) * N_SEL + k * kstep + kk], 0, g, 0, 0))

    b4 = lambda b, k, *_: (b, 0, 0, 0)
    b5 = lambda b, k, *_: (b, 0, 0, 0, 0)
    grid_spec = pltpu.PrefetchScalarGridSpec(
        num_scalar_prefetch=3,
        grid=(DB, N_SEL // kstep),
        in_specs=[page_spec(g, kk) for g in range(N_KV) for kk in range(kstep)] + [
            pl.BlockSpec((1, N_KV, JPAD, HEAD_DIM), b4),
            pl.BlockSpec((1, N_KV, JPAD, N_BRANCH), b4),
            pl.BlockSpec((1, 2, N_KV, HEAD_DIM, 1), b5),
            pl.BlockSpec((1, 2, N_KV, HEAD_DIM, n_win), b5),
            pl.BlockSpec((1, 2, N_KV, HEAD_DIM, 1), b5),
            pl.BlockSpec((1, N_KV, JPAD, HEAD_DIM), b4),
        ],
        out_specs=pl.BlockSpec((1, N_KV, JPAD, HEAD_DIM), b4),
        scratch_shapes=[pltpu.VMEM((N_KV, JPAD, 1), F32), pltpu.VMEM((N_KV, JPAD, 1), F32),
                        pltpu.VMEM((N_KV, JPAD, HEAD_DIM), F32)],
    )
    return pl.pallas_call(
        functools.partial(_sample_sel_kernel, qpos=qpos, nb_past=nb_past, kstep=kstep),
        grid_spec=grid_spec,
        out_shape=jax.ShapeDtypeStruct((DB, N_KV, JPAD, HEAD_DIM), F32),
        compiler_params=_cparams(("arbitrary", "arbitrary")),
        name="sample_sel_win",
    )(pages, idx, ok, *([cache_t] * (N_KV * kstep)), q4, gate4, sel_new_t, win_t, win_new_t, o_c)


def _imp_matrix(n_cmp_rows, n_blk_cols):
    m = np.zeros((n_cmp_rows, n_blk_cols), np.float32)
    for j in range(n_blk_cols):
        for a in range(SEL_RATIO):
            for c in range(CMP_PARTS):
                i = SEL_RATIO * j + a - c
                if 0 <= i < n_cmp_rows:
                    m[i, j] += 1.0
    return m


def _rope_angles(pos):
    inv = ROPE_THETA ** (-jnp.arange(0, ROT_DIM, 2, dtype=F32) / ROT_DIM)
    ang = pos.astype(F32)[:, None] * inv[None, :]
    return jnp.cos(ang), jnp.sin(ang)


def _rope_tables(pos, period):
    cos, sin = _rope_angles(pos)
    n = pos.shape[0]
    rest1, rest0, zh = jnp.ones((n, period - ROT_DIM), F32), jnp.zeros((n, period - ROT_DIM), F32), jnp.zeros_like(sin)
    c = jnp.concatenate([cos, cos, rest1], axis=1)
    sa = jnp.concatenate([-sin, zh, rest0], axis=1)
    sb = jnp.concatenate([zh, sin, rest0], axis=1)
    return tuple(jnp.tile(t, (1, LANE // period)) for t in (c, sa, sb))


def _to_rows(kv_t):
    return jnp.transpose(kv_t, (0, 4, 1, 2, 3))


def _to_cols(kv):
    return jnp.transpose(kv, (0, 2, 3, 4, 1))


def kernel(x_prompt, x_sample, state_pool, cache_cmp_kv, cache_sel_kv, state_win_kv, page_table,
           norm_mix, norm_ffn, pool_w, pool_scale, w_qg, w_o, norm_kv, w_kv, cmp_pe, cmp_w1, cmp_w2,
           mlp_up, mlp_down, norm_final):
    B, T, D = x_prompt.shape
    DB, S, _ = x_sample.shape
    past_len = page_table.shape[1] * PAGE_SIZE
    n_blk_p = T // L_SEL
    assert S == 1 and D == D_MODEL and T % PAGE_SIZE == 0 and n_blk_p <= HEAD_SLOT - HEAD_DIM and n_blk_p % SUBLANE == 0

    n_q = N_HEADS * HEAD_DIM
    w_q = w_qg[0][:, :n_q]
    w_gate = jnp.pad(w_qg[0][:, n_q:], ((0, 0), (0, GATE_LANES - N_HEADS * N_BRANCH)))
    w_kv5 = w_kv.reshape(D, N_BRANCH, 2, N_KV * HEAD_DIM)
    w_k_sel_win = jnp.concatenate([w_kv5[:, 1, 0], w_kv5[:, 2, 0]], axis=1).astype(BF16)
    w_kv_b = w_kv.astype(BF16)
    w_o_b = w_o[0].astype(BF16)
    pool_w_b = pool_w[0].astype(BF16)
    up_b, down_b = mlp_up.astype(BF16), mlp_down.astype(BF16)
    g_mix0, g_mix1 = norm_mix[0:1], norm_mix[1:2]
    g_kv, g_fin = norm_kv[None, :], norm_final[None, :]
    w1_6 = cmp_w1.reshape(2, CMP_PARTS, STRIDE // 2, 2, HEAD_DIM, CMP_HIDDEN)
    w1pair = jnp.einsum('ab,eprshk->ersahbpk', jnp.eye(2, dtype=F32), w1_6)
    w1pair = w1pair.reshape(2, STRIDE // 2, 2 * 2 * HEAD_DIM, 2 * CMP_PARTS * CMP_HIDDEN).astype(BF16)
    w1_flat = cmp_w1.reshape(2, L_CMP * HEAD_DIM, CMP_HIDDEN).astype(BF16)
    pe8 = jnp.broadcast_to(cmp_pe.transpose(1, 0, 2).reshape(2, 1, L_CMP * HEAD_DIM), (2, SUBLANE, L_CMP * HEAD_DIM)).astype(BF16)
    w2pad = jnp.pad(cmp_w2[0], ((0, 0), (0, HEAD_SLOT - HEAD_DIM))).astype(BF16)
    w2t = cmp_w2.transpose(0, 2, 1).astype(BF16)

    h1, pool_tail = _pool_prompt(x_prompt, g_mix0, pool_w_b, pool_scale)
    pool_prompt = pool_tail[None, :, POOL_HALO - POOL_BUF:]
    h2 = _mlp(h1.reshape(B * T, D), norm_ffn[0:1], up_b[0], down_b[0], g_fin)
    pos_p = jnp.arange(T)
    cos_p, sin_p = _rope_angles(pos_p)
    cmp_t, sel_t, win_t, ksel, kwin, vsel, vwin, q_pad, gate_t = _proj_prompt(
        h2.reshape(B, T, D), g_kv, g_mix1, w_kv_b.T, w_k_sel_win, w_q.astype(BF16), w_gate.T.astype(BF16),
        _rope_tables(pos_p, HEAD_DIM), (cos_p.T, sin_p.T))
    kc_nat_p, _, vc_t_p = _cmp_finish(_cmp_parts(cmp_t, w1pair), pe8, w1_flat, w2pad, w2t)
    imp_t = jnp.asarray(_imp_matrix(kc_nat_p.shape[2], n_blk_p).T, BF16)
    o_p = _attn_prompt(q_pad, gate_t, kc_nat_p, vc_t_p, ksel, vsel, kwin, vwin, imp_t)
    y_p = _mlp(h2, norm_ffn[1:2], up_b[1], down_b[1], g_fin, o=o_p.reshape(B * T, D), w_o=w_o_b, final_norm=True)

    hist = jnp.concatenate([jnp.zeros((1, DB, D), F32), state_pool[0].transpose(1, 0, 2)], axis=0)
    hs1, us = _pool_sample(x_sample[:, 0], hist, g_mix0, pool_w_b, pool_scale)
    pool_sample = jnp.concatenate([state_pool[0][:, 1:], us[:, None]], axis=1)[None]
    hs2 = _mlp(hs1, norm_ffn[0:1], up_b[0], down_b[0], g_fin)
    cmp_s, sel_s, win_s, q_s, gate_s = _proj_sample(
        hs2, g_kv, g_mix1, w_kv_b, w_q.astype(BF16), w_gate.astype(BF16),
        _rope_tables(jnp.full((DB,), past_len, jnp.int32), HEAD_DIM))
    parts_s = _cmp_parts(_to_cols(cache_cmp_kv), w1pair, pages=page_table.reshape(-1), n_b=DB)
    _, kc_t_s, vc_t_s = _cmp_finish(parts_s, pe8, w1_flat, w2pad, w2t)
    n_valid = past_len // STRIDE - CMP_PARTS + 1
    nb_past = past_len // L_SEL
    n_blk_lanes = -(-(nb_past + 1) // LANE) * LANE
    jpad = ((0, 0), (0, 0), (0, JPAD - HPG), (0, 0))
    q4 = jnp.pad(q_s.astype(F32).reshape(DB, N_KV, HPG, HEAD_DIM), jpad)
    gate4 = jnp.pad(gate_s[:, :N_HEADS * N_BRANCH].reshape(DB, N_KV, HPG, N_BRANCH), jpad)
    imp_s = jnp.asarray(_imp_matrix(kc_t_s.shape[3], n_blk_lanes), BF16)
    o_c, idx_pad, val_pad = _sample_cmp(q4, kc_t_s, vc_t_s, imp_s, n_valid, nb_past)
    idx = idx_pad[:, :N_SEL].reshape(DB, N_KV, N_SEL)
    ok = (val_pad[:, :N_SEL] >= 0).astype(jnp.int32).reshape(DB, N_KV, N_SEL)
    bpp = PAGE_SIZE // L_SEL
    logical_page = (jnp.minimum(idx, nb_past - 1) // bpp).reshape(DB, -1)
    pages = jnp.take_along_axis(page_table, logical_page, axis=1)
    kv5 = lambda a, n, t: a.reshape(n, t, 2, N_KV, HEAD_DIM)
    o_s = _sample_sel(pages.reshape(-1), idx.reshape(-1), ok.reshape(-1), _to_cols(cache_sel_kv), q4, gate4,
                      _to_cols(kv5(sel_s, DB, 1)), _to_cols(state_win_kv), _to_cols(kv5(win_s, DB, 1)), o_c,
                      past_len, nb_past)
    o_s = o_s[:, :, :HPG].reshape(DB, D).astype(BF16)
    y_s = _mlp(hs2, norm_ffn[1:2], up_b[1], down_b[1], g_fin, o=o_s, w_o=w_o_b, final_norm=True)

    win_sample = jnp.concatenate([state_win_kv, kv5(win_s, DB, 1)], axis=1)[:, S:]
    return (y_p.reshape(B, T, D), y_s.reshape(DB, S, D),
            _to_rows(cmp_t), _to_rows(sel_t), _to_rows(win_t[..., T - min(WINDOW, T):]), pool_prompt,
            kv5(cmp_s, DB, 1), kv5(sel_s, DB, 1), win_sample, pool_sample)
```

```python
import functools
import math

import numpy as np
import jax
import jax.numpy as jnp
from jax import lax
from jax.experimental import pallas as pl
from jax.experimental.pallas import tpu as pltpu

D_MODEL = 1024
POOL_WINDOWS = (2, 4, 8, 16)
POOL_GROUP = D_MODEL // len(POOL_WINDOWS)
POOL_BUF = max(POOL_WINDOWS) - 1
POOL_HALO = 16
N_HEADS = 16
N_KV = 4
HPG = N_HEADS // N_KV
HEAD_DIM = 64
KV_LANES = N_KV * HEAD_DIM
ROW_LANES = 2 * KV_LANES
ROT_DIM = HEAD_DIM // 4
ROT_HALF = ROT_DIM // 2
ROPE_THETA = 500000.0
L_CMP = 32
STRIDE = 16
CMP_HIDDEN = 2 * HEAD_DIM
L_SEL = 64
SEL_RATIO = L_SEL // STRIDE
CMP_PARTS = L_CMP // STRIDE
N_SEL = 16
WINDOW = 512
N_BRANCH = 3
D_FF = 4 * D_MODEL
RMS_EPS = 1e-6
NEG_INF = -1e30
FORCE_SCORE = 1e9
Q_SCALE_LOG2 = HEAD_DIM ** -0.5 * math.log2(math.e)
PAGE_SIZE = 128
SEG_PER_PAGE = PAGE_SIZE // STRIDE
GATE_LANES = 128
LANE = 128
SUBLANE = 8
HEAD_SLOT = LANE
JPAD = SUBLANE
ONES_ROWS = 2 * SUBLANE

VMEM_LIMIT = 56 * 1024 * 1024

F32 = jnp.float32
BF16 = jnp.bfloat16


def _cparams(sem):
    return pltpu.CompilerParams(dimension_semantics=sem, vmem_limit_bytes=VMEM_LIMIT)


def _dot(a, b):
    return jnp.dot(a, b, preferred_element_type=F32)


def _dot_nt(a, b):
    return lax.dot_general(a, b, (((1,), (1,)), ((), ())), preferred_element_type=F32)


def _split3(x):
    hi = x.astype(BF16)
    r1 = x - hi.astype(F32)
    mid = r1.astype(BF16)
    lo = (r1 - mid.astype(F32)).astype(BF16)
    return hi, mid, lo


def _dot_exact_lhs(x, m_bf16):
    hi, mid, lo = _split3(x)
    return _dot(hi, m_bf16) + _dot(mid, m_bf16) + _dot(lo, m_bf16)


def _dot_exact_rhs(m_bf16, x):
    hi, mid, lo = _split3(x)
    return _dot(m_bf16, hi) + _dot(m_bf16, mid) + _dot(m_bf16, lo)


def _rms_scale(x):
    return x * lax.rsqrt(jnp.mean(x * x, axis=-1, keepdims=True) + RMS_EPS)


def _rope_lanes(x, c, sa, sb):
    outs = []
    for t in range(x.shape[1] // LANE):
        xc = x[:, t * LANE:(t + 1) * LANE]
        outs.append(xc * c + pltpu.roll(xc, LANE - ROT_HALF, axis=1) * sa + pltpu.roll(xc, ROT_HALF, axis=1) * sb)
    return jnp.concatenate(outs, axis=1)


def _masked_softmax(s, mask, axis, base2=False):
    sm = jnp.where(mask, s, NEG_INF)
    m = jnp.max(sm, axis=axis, keepdims=True)
    e = jnp.where(mask, (jnp.exp2 if base2 else jnp.exp)(sm - m), 0.0)
    l = jnp.sum(e, axis=axis, keepdims=True)
    return e * (1.0 / jnp.where(l > 0.0, l, 1.0))


def _pool_prompt_kernel(x_ref, xprev_ref, g_ref, w_ref, scale_ref, h_ref, buf_ref, *, tt):
    i = pl.program_id(1)
    x = x_ref[0]
    g = g_ref[...]
    u = _rms_scale(x) * g
    up = _rms_scale(xprev_ref[0]) * g
    up = jnp.where(i > 0, up, 0.0)
    ext = jnp.concatenate([up, u], axis=0)
    pos = i * tt + lax.broadcasted_iota(jnp.int32, (tt, 1), 0)
    scale = scale_ref[...]
    for gi, w in enumerate(POOL_WINDOWS):
        sl = slice(gi * POOL_GROUP, (gi + 1) * POOL_GROUP)
        s = ext[:, sl]
        k = 1
        while k < w:
            s = s + pltpu.roll(s, k, axis=0)
            k *= 2
        s = s[POOL_HALO:]
        cnt = jnp.minimum(pos + 1, w).astype(F32)
        d = s / cnt - u[:, sl]
        z = _dot(d.astype(BF16), w_ref[gi])
        h_ref[0, :, sl] = x[:, sl] + z * scale[:, sl]
    buf_ref[0] = u[tt - POOL_HALO:]


def _pool_prompt(x, g_mix, w_pool_bf16, scale, tt=512):
    B, T, D = x.shape
    hb = tt // POOL_HALO
    return pl.pallas_call(
        functools.partial(_pool_prompt_kernel, tt=tt),
        grid=(B, T // tt),
        in_specs=[
            pl.BlockSpec((1, tt, D), lambda b, i: (b, i, 0)),
            pl.BlockSpec((1, POOL_HALO, D), lambda b, i: (b, jnp.maximum(i * hb - 1, 0), 0)),
            pl.BlockSpec((1, D), lambda b, i: (0, 0)),
            pl.BlockSpec((len(POOL_WINDOWS), POOL_GROUP, POOL_GROUP), lambda b, i: (0, 0, 0)),
            pl.BlockSpec((1, D), lambda b, i: (0, 0)),
        ],
        out_specs=[
            pl.BlockSpec((1, tt, D), lambda b, i: (b, i, 0)),
            pl.BlockSpec((1, POOL_HALO, D), lambda b, i: (b, 0, 0)),
        ],
        out_shape=[jax.ShapeDtypeStruct((B, T, D), F32), jax.ShapeDtypeStruct((B, POOL_HALO, D), F32)],
        compiler_params=_cparams(("arbitrary", "arbitrary")),
        name="pool_prompt",
    )(x, x, g_mix, w_pool_bf16, scale)


def _pool_sample_kernel(x_ref, hist_ref, g_ref, w_ref, scale_ref, h_ref, u_ref):
    x = x_ref[...]
    u = _rms_scale(x) * g_ref[...]
    u_ref[...] = u
    scale = scale_ref[...]
    for gi, w in enumerate(POOL_WINDOWS):
        sl = slice(gi * POOL_GROUP, (gi + 1) * POOL_GROUP)
        s = u[:, sl]
        for r in range(1, w):
            s = s + hist_ref[POOL_BUF + 1 - r][:, sl]
        d = s / float(w) - u[:, sl]
        z = _dot(d.astype(BF16), w_ref[gi])
        h_ref[:, sl] = x[:, sl] + z * scale[:, sl]


def _pool_sample(x, hist, g_mix, w_pool_bf16, scale):
    DB, D = x.shape
    return pl.pallas_call(
        _pool_sample_kernel,
        out_shape=[jax.ShapeDtypeStruct((DB, D), F32), jax.ShapeDtypeStruct((DB, D), F32)],
        compiler_params=pltpu.CompilerParams(vmem_limit_bytes=VMEM_LIMIT),
        name="pool_sample",
    )(x, hist, g_mix, w_pool_bf16, scale)


def _mlp_kernel(*refs, has_attn, final_norm):
    if has_attn:
        h_ref, o_ref, wo_ref, g_ref, wup_ref, wdn_ref, gfin_ref, out_ref, hres, xn, acc = refs
    else:
        h_ref, g_ref, wup_ref, wdn_ref, gfin_ref, out_ref, hres, xn, acc = refs
    f = pl.program_id(1)

    @pl.when(f == 0)
    def _():
        h = h_ref[...]
        if has_attn:
            h = h + _dot(o_ref[...], wo_ref[...])
        hres[...] = h
        xn[...] = (_rms_scale(h) * g_ref[...]).astype(BF16)
        acc[...] = jnp.zeros_like(acc)

    a = jnp.maximum(_dot(xn[...], wup_ref[...]), 0.0)
    acc[...] += _dot((a * a).astype(BF16), wdn_ref[...])

    @pl.when(f == pl.num_programs(1) - 1)
    def _():
        y = hres[...] + acc[...]
        if final_norm:
            y = _rms_scale(y) * gfin_ref[...]
        out_ref[...] = y


def _mlp(h, g_ffn, w_up, w_down, g_final, o=None, w_o=None, final_norm=False, tm=1024, tf=1024):
    M, D = h.shape
    tm = min(tm, M)
    has_attn = o is not None
    row = lambda i, f: (i, 0)
    const = lambda i, f: (0, 0)
    in_specs = [pl.BlockSpec((tm, D), row)]
    args = [h]
    if has_attn:
        in_specs += [pl.BlockSpec((tm, D), row), pl.BlockSpec((D, D), const)]
        args += [o, w_o]
    in_specs += [
        pl.BlockSpec((1, D), const),
        pl.BlockSpec((D, tf), lambda i, f: (0, f)),
        pl.BlockSpec((tf, D), lambda i, f: (f, 0)),
        pl.BlockSpec((1, D), const),
    ]
    args += [g_ffn, w_up, w_down, g_final]
    return pl.pallas_call(
        functools.partial(_mlp_kernel, has_attn=has_attn, final_norm=final_norm),
        grid=(M // tm, D_FF // tf),
        in_specs=in_specs,
        out_specs=pl.BlockSpec((tm, D), row),
        out_shape=jax.ShapeDtypeStruct((M, D), F32),
        scratch_shapes=[pltpu.VMEM((tm, D), F32), pltpu.VMEM((tm, D), BF16), pltpu.VMEM((tm, D), F32)],
        compiler_params=_cparams(("arbitrary", "arbitrary")),
        name="mlp",
    )(*args)


def _proj_prompt_kernel(h_ref, gkv_ref, gq_ref, wkvt_ref, wkp_ref, wqp_ref, wgt_ref, c_ref, sa_ref, sb_ref, ct_ref, st_ref,
                        cmp_ref, sel_ref, win_ref, ksel_ref, kwin_ref, vsel_ref, vwin_ref, q_ref, gate_ref, *, tt):
    i = pl.program_id(1)
    y = _rms_scale(h_ref[0])
    xkv = (y * gkv_ref[...]).astype(BF16)
    xq = (y * gq_ref[...]).astype(BF16)

    kvt = _dot_nt(wkvt_ref[...], xkv)
    cos_t, sin_t = ct_ref[...], st_ref[...]
    for br, (oref, vref) in enumerate(((cmp_ref, None), (sel_ref, vsel_ref), (win_ref, vwin_ref))):
        base = br * ROW_LANES
        for g in range(N_KV):
            r0 = base + g * HEAD_DIM
            x1, x2 = kvt[r0:r0 + ROT_HALF], kvt[r0 + ROT_HALF:r0 + ROT_DIM]
            oref[0, 0, g] = jnp.concatenate(
                [x1 * cos_t - x2 * sin_t, x2 * cos_t + x1 * sin_t, kvt[r0 + ROT_DIM:r0 + HEAD_DIM]], axis=0)
            v = kvt[r0 + KV_LANES:r0 + KV_LANES + HEAD_DIM]
            oref[0, 1, g] = v
            if vref is not None:
                for c in range(tt // LANE):
                    vref[0, g, c] = v[:, c * LANE:(c + 1) * LANE].astype(BF16)

    c, sa, sb = c_ref[...], sa_ref[...], sb_ref[...]
    kp = _rope_lanes(_dot(xkv, wkp_ref[...]), c, sa, sb)
    pos = i * tt + lax.broadcasted_iota(jnp.int32, (tt, 1), 0)
    half_lane = lax.broadcasted_iota(jnp.int32, (1, HEAD_SLOT - HEAD_DIM), 1)
    blk_onehot = jnp.where(half_lane == pos // L_SEL, 1.0, 0.0)
    no_bias = jnp.zeros((tt, HEAD_SLOT - HEAD_DIM), F32)

    def head(x, h):
        return x[:, h * HEAD_DIM:(h + 1) * HEAD_DIM]

    for g in range(N_KV):
        ksel_ref[0, g] = jnp.concatenate([head(kp, g), blk_onehot], axis=1).astype(BF16)
        kwin_ref[0, g] = jnp.concatenate([head(kp, N_KV + g), no_bias], axis=1).astype(BF16)

    q = _rope_lanes(_dot(xq, wqp_ref[...]), c, sa, sb) * Q_SCALE_LOG2
    q_ref[0] = jnp.concatenate([piece for h in range(N_HEADS) for piece in (head(q, h), no_bias)], axis=1).astype(BF16)
    gate_ref[0] = jax.nn.sigmoid(_dot_nt(wgt_ref[...], xq))


def _proj_prompt(h, g_kv, g_q, w_kvt, w_k, w_q, w_gt, rope_nat, rope_t, tt=512):
    B, T, D = h.shape
    const2 = lambda b, i: (0, 0)
    kv_t = jax.ShapeDtypeStruct((B, 2, N_KV, HEAD_DIM, T), F32)
    k_nat = jax.ShapeDtypeStruct((B, N_KV, T, HEAD_SLOT), BF16)
    v_t = jax.ShapeDtypeStruct((B, N_KV, T // LANE, HEAD_DIM, LANE), BF16)
    kv_spec = pl.BlockSpec((1, 2, N_KV, HEAD_DIM, tt), lambda b, i: (b, 0, 0, 0, i))
    k_spec = pl.BlockSpec((1, N_KV, tt, HEAD_SLOT), lambda b, i: (b, 0, i, 0))
    v_spec = pl.BlockSpec((1, N_KV, tt // LANE, HEAD_DIM, LANE), lambda b, i: (b, 0, i, 0, 0))
    return pl.pallas_call(
        functools.partial(_proj_prompt_kernel, tt=tt),
        grid=(B, T // tt),
        in_specs=[
            pl.BlockSpec((1, tt, D), lambda b, i: (b, i, 0)),
            pl.BlockSpec((1, D), const2), pl.BlockSpec((1, D), const2),
            pl.BlockSpec(w_kvt.shape, const2), pl.BlockSpec(w_k.shape, const2),
            pl.BlockSpec(w_q.shape, const2), pl.BlockSpec(w_gt.shape, const2),
            pl.BlockSpec((tt, LANE), lambda b, i: (i, 0)), pl.BlockSpec((tt, LANE), lambda b, i: (i, 0)),
            pl.BlockSpec((tt, LANE), lambda b, i: (i, 0)),
            pl.BlockSpec((ROT_HALF, tt), lambda b, i: (0, i)), pl.BlockSpec((ROT_HALF, tt), lambda b, i: (0, i)),
        ],
        out_specs=[kv_spec, kv_spec, kv_spec, k_spec, k_spec, v_spec, v_spec,
                   pl.BlockSpec((1, tt, N_HEADS * HEAD_SLOT), lambda b, i: (b, i, 0)),
                   pl.BlockSpec((1, GATE_LANES, tt), lambda b, i: (b, 0, i))],
        out_shape=[kv_t, kv_t, kv_t, k_nat, k_nat, v_t, v_t,
                   jax.ShapeDtypeStruct((B, T, N_HEADS * HEAD_SLOT), BF16),
                   jax.ShapeDtypeStruct((B, GATE_LANES, T), F32)],
        compiler_params=_cparams(("arbitrary", "arbitrary")),
        name="kv_q_proj_prompt",
    )(h, g_kv, g_q, w_kvt, w_k, w_q, w_gt, *rope_nat, *rope_t)


def _proj_sample_kernel(h_ref, gkv_ref, gq_ref, wkv_ref, wq_ref, wg_ref, c_ref, sa_ref, sb_ref,
                        cmp_ref, sel_ref, win_ref, q_ref, gate_ref):
    y = _rms_scale(h_ref[...])
    xkv = (y * gkv_ref[...]).astype(BF16)
    xq = (y * gq_ref[...]).astype(BF16)
    c, sa, sb = c_ref[...], sa_ref[...], sb_ref[...]
    kv = _dot(xkv, wkv_ref[...])
    for br, oref in enumerate((cmp_ref, sel_ref, win_ref)):
        oref[:, 0:KV_LANES] = _rope_lanes(kv[:, br * ROW_LANES:br * ROW_LANES + KV_LANES], c, sa, sb)
        oref[:, KV_LANES:ROW_LANES] = kv[:, br * ROW_LANES + KV_LANES:(br + 1) * ROW_LANES]
    q_ref[...] = (_rope_lanes(_dot(xq, wq_ref[...]), c, sa, sb) * (HEAD_DIM ** -0.5)).astype(BF16)
    gate_ref[...] = jax.nn.sigmoid(_dot(xq, wg_ref[...]))


def _proj_sample(h, g_kv, g_q, w_kv, w_q, w_g, rope_nat):
    M, D = h.shape
    return pl.pallas_call(
        _proj_sample_kernel,
        out_shape=[jax.ShapeDtypeStruct((M, ROW_LANES), F32)] * 3
                  + [jax.ShapeDtypeStruct((M, D), BF16), jax.ShapeDtypeStruct((M, GATE_LANES), F32)],
        compiler_params=pltpu.CompilerParams(vmem_limit_bytes=VMEM_LIMIT),
        name="kv_q_proj_sample",
    )(h, g_kv, g_q, w_kv, w_q, w_g, *rope_nat)


def _parts_kernel(*refs, pg, prefetch):
    refs = refs[prefetch:]
    x_refs, w_ref, out_ref, xs = refs[:pg], refs[pg], refs[pg + 1], refs[pg + 2]
    n_rows = pg * SEG_PER_PAGE
    stages = [(e, gp) for e in range(2) for gp in range(N_KV // 2)]

    def fill(k):
        e, gp = stages[k]
        half = STRIDE // SUBLANE
        for p, xr in enumerate(x_refs):
            xt = jnp.concatenate([xr[0, e, 2 * gp], xr[0, e, 2 * gp + 1]], axis=0)
            rows = xt.T.reshape(SEG_PER_PAGE, half, SUBLANE, LANE).transpose(1, 2, 0, 3).reshape(STRIDE, SEG_PER_PAGE, LANE)
            for r in range(STRIDE):
                xs[k % 2, r, p * SEG_PER_PAGE:(p + 1) * SEG_PER_PAGE, :] = rows[r]

    fill(0)
    for k, (e, gp) in enumerate(stages):
        if k + 1 < len(stages):
            fill(k + 1)
        acc = jnp.zeros((n_rows, 2 * 2 * CMP_HIDDEN), F32)
        for rp in range(STRIDE // 2):
            lhs = jnp.concatenate([xs[k % 2, 2 * rp], xs[k % 2, 2 * rp + 1]], axis=1)
            acc = acc + _dot(lhs.astype(BF16), w_ref[e, rp])
        c0 = (e * N_KV + 2 * gp) * 2 * CMP_HIDDEN
        out_ref[0, :, c0:c0 + 2 * 2 * CMP_HIDDEN] = acc


def _cmp_parts(kv_t, w1pair, pages=None, n_b=None, pg=16):
    blk = (1, 2, N_KV, HEAD_DIM, PAGE_SIZE)
    width = 2 * N_KV * 2 * CMP_HIDDEN
    if pages is None:
        n_b, ppb = kv_t.shape[0], kv_t.shape[-1] // PAGE_SIZE
        pg = min(pg, ppb)
        in_specs = [pl.BlockSpec(blk, functools.partial(lambda b, s, p: (b, 0, 0, 0, s * pg + p), p=p)) for p in range(pg)]
        in_specs.append(pl.BlockSpec(w1pair.shape, lambda b, s: (0, 0, 0, 0)))
        out_spec = pl.BlockSpec((1, pg * SEG_PER_PAGE, width), lambda b, s: (b, s, 0))
        grid_kw = dict(grid=(n_b, ppb // pg), in_specs=in_specs, out_specs=out_spec,
                       scratch_shapes=[pltpu.VMEM((2, STRIDE, pg * SEG_PER_PAGE, LANE), F32)])
        args = [kv_t] * pg + [w1pair]
        prefetch = 0
    else:
        ppb = pages.shape[0] // n_b
        pg = min(pg, ppb)
        in_specs = [pl.BlockSpec(blk, functools.partial(lambda b, s, pr, p: (pr[b * ppb + s * pg + p], 0, 0, 0, 0), p=p))
                    for p in range(pg)]
        in_specs.append(pl.BlockSpec(w1pair.shape, lambda b, s, pr: (0, 0, 0, 0)))
        out_spec = pl.BlockSpec((1, pg * SEG_PER_PAGE, width), lambda b, s, pr: (b, s, 0))
        grid_kw = dict(grid_spec=pltpu.PrefetchScalarGridSpec(
            num_scalar_prefetch=1, grid=(n_b, ppb // pg), in_specs=in_specs, out_specs=out_spec,
            scratch_shapes=[pltpu.VMEM((2, STRIDE, pg * SEG_PER_PAGE, LANE), F32)]))
        args = [pages] + [kv_t] * pg + [w1pair]
        prefetch = 1
    return pl.pallas_call(
        functools.partial(_parts_kernel, pg=pg, prefetch=prefetch),
        out_shape=jax.ShapeDtypeStruct((n_b, ppb * SEG_PER_PAGE, width), F32),
        compiler_params=_cparams(("arbitrary", "arbitrary")),
        name="cmp_parts",
        **grid_kw,
    )(*args)


def _cmp_finish_kernel(parts_ref, pe_ref, w1f_ref, w2p_ref, w2t_ref, knat_ref, kt_ref, vt_ref, *, n_seg):
    rows = lax.broadcasted_iota(jnp.int32, (n_seg, 1), 0)
    for e in range(2):
        pe_sum = _dot(pe_ref[e], w1f_ref[e])[0:1]
        for g in range(N_KV):
            c0 = (e * N_KV + g) * 2 * CMP_HIDDEN
            first = parts_ref[0, :, c0:c0 + CMP_HIDDEN]
            second = pltpu.roll(parts_ref[0, :, c0 + CMP_HIDDEN:c0 + 2 * CMP_HIDDEN], n_seg - 1, axis=0)
            act = jax.nn.gelu(first + second + pe_sum)
            act = jnp.where(rows < n_seg - 1, act, 0.0).astype(BF16)
            o_t = _dot_nt(w2t_ref[e], act).astype(BF16)
            if e == 0:
                knat_ref[0, g] = _dot(act, w2p_ref[...]).astype(BF16)
                kt_ref[0, g] = o_t
            else:
                vt_ref[0, g] = o_t


def _cmp_finish(parts, pe8, w1f, w2pad, w2t):
    n_b, n_seg, width = parts.shape
    c3 = lambda b: (0, 0, 0)
    return pl.pallas_call(
        functools.partial(_cmp_finish_kernel, n_seg=n_seg),
        grid=(n_b,),
        in_specs=[
            pl.BlockSpec((1, n_seg, width), lambda b: (b, 0, 0)),
            pl.BlockSpec(pe8.shape, c3), pl.BlockSpec(w1f.shape, c3),
            pl.BlockSpec(w2pad.shape, lambda b: (0, 0)), pl.BlockSpec(w2t.shape, c3),
        ],
        out_specs=[pl.BlockSpec((1, N_KV, n_seg, HEAD_SLOT), lambda b: (b, 0, 0, 0)),
                   pl.BlockSpec((1, N_KV, HEAD_DIM, n_seg), lambda b: (b, 0, 0, 0)),
                   pl.BlockSpec((1, N_KV, HEAD_DIM, n_seg), lambda b: (b, 0, 0, 0))],
        out_shape=[jax.ShapeDtypeStruct((n_b, N_KV, n_seg, HEAD_SLOT), BF16),
                   jax.ShapeDtypeStruct((n_b, N_KV, HEAD_DIM, n_seg), BF16),
                   jax.ShapeDtypeStruct((n_b, N_KV, HEAD_DIM, n_seg), BF16)],
        compiler_params=_cparams(("arbitrary",)),
        name="cmp_finish",
    )(parts, pe8, w1f, w2pad, w2t)


def _topk_rows(score, n_sel, n_live):
    n_blk = score.shape[0]
    n_tiles = n_blk // SUBLANE
    tiles = [score[r * SUBLANE:(r + 1) * SUBLANE] for r in range(n_tiles)]
    sub = lax.broadcasted_iota(jnp.int32, (SUBLANE, 1), 0)

    def count_tile(rank, ri):
        rank = list(rank)
        for i in range(ri * SUBLANE, (ri + 1) * SUBLANE):
            bi = jnp.broadcast_to(score[i:i + 1], tiles[0].shape)
            for r in range(n_tiles):
                if r > ri:
                    ahead = jnp.where(bi >= tiles[r], 1.0, 0.0)
                elif r < ri:
                    ahead = jnp.where(bi > tiles[r], 1.0, 0.0)
                else:
                    tie = jnp.where(sub > i % SUBLANE, 1.0, 0.0)
                    ahead = jnp.where(bi > tiles[r], 1.0, jnp.where(bi == tiles[r], tie, 0.0))
                rank[r] = rank[r] + ahead
        return tuple(rank)

    rank = tuple(jnp.zeros(tiles[0].shape, F32) for _ in range(n_tiles))
    for ri in range(n_tiles):
        rank = lax.cond(ri * SUBLANE < n_live, functools.partial(count_tile, ri=ri), lambda r: r, rank)
    return jnp.concatenate(rank, axis=0) < float(n_sel)


def _attn_prompt_kernel(q_ref, gt_ref, kc_ref, vc_ref, ks_ref, vs_ref, kw_ref, vw_ref, impt_ref, eye_ref, o_ref,
                        *, tq, kc, t_len):
    i = pl.program_id(1)
    s0 = i * tq
    n_cmp = kc_ref.shape[2]
    n_blk = t_len // L_SEL
    rows = HPG * tq
    qpos = s0 + lax.broadcasted_iota(jnp.int32, (1, tq), 1)
    qpos4 = jnp.concatenate([qpos] * HPG, axis=1)
    blk = lax.broadcasted_iota(jnp.int32, (n_blk, 1), 0)
    cur = qpos // L_SEL
    valid = blk <= cur
    forced = (blk == 0) | (blk == cur) | (blk == cur - 1)
    c_end = lax.broadcasted_iota(jnp.int32, (n_cmp, 1), 0) * STRIDE + (L_CMP - 1)
    cmask = c_end <= qpos4
    slot_lane = lax.broadcasted_iota(jnp.int32, (1, HEAD_SLOT), 1)
    n_full = s0 // kc
    w_len = WINDOW + tq
    w0 = pl.multiple_of(jnp.clip(s0 - WINDOW, 0, t_len - w_len), LANE)
    wpos = w0 + lax.broadcasted_iota(jnp.int32, (w_len, 1), 0)
    wmask = (wpos <= qpos4) & (wpos >= qpos4 - WINDOW)
    gt = gt_ref[0]
    heads = []

    def with_ones(v):
        return jnp.concatenate([v, jnp.ones((ONES_ROWS, v.shape[1]), BF16)], axis=0)

    def flash_update(s, v, carry):
        m_i, acc = carry
        m_new = jnp.maximum(m_i, jnp.max(s, axis=0, keepdims=True))
        e = jnp.exp2(s - m_new)
        return m_new, jnp.exp2(m_i - m_new) * acc + _dot(v, e.astype(BF16))

    def normalise(acc):
        return acc[0:HEAD_DIM] * (1.0 / acc[HEAD_DIM:HEAD_DIM + 1])

    def gate_row(g, br):
        return jnp.concatenate([gt[(g * HPG + j) * N_BRANCH + br:(g * HPG + j) * N_BRANCH + br + 1] for j in range(HPG)], axis=1)

    for gp in range(N_KV // 2):
        gs = (2 * gp, 2 * gp + 1)
        qg = [jnp.concatenate([q_ref[0, :, (g * HPG + j) * HEAD_SLOT:(g * HPG + j + 1) * HEAD_SLOT] for j in range(HPG)], axis=0)
              for g in gs]

        s_c = [_dot_nt(kc_ref[0, g], qg[t]) for t, g in enumerate(gs)]
        p_c = [_masked_softmax(s, cmask, 0, base2=True) for s in s_c]
        o_c = [_dot(vc_ref[0, g], p_c[t].astype(BF16)) for t, g in enumerate(gs)]
        qa = []
        for t in range(2):
            p_sum = p_c[t][:, 0:tq]
            for j in range(1, HPG):
                p_sum = p_sum + p_c[t][:, j * tq:(j + 1) * tq]
            imp = _dot_exact_rhs(impt_ref[...], p_sum)
            score = jnp.where(forced, FORCE_SCORE, jnp.where(valid, imp, -1.0))
            sel = _topk_rows(score, min(N_SEL, n_blk), (s0 + tq - 1) // L_SEL + 1) & valid
            pieces = [jnp.ones((HEAD_DIM, tq), F32), jnp.where(sel, 1.0, 0.0)]
            if n_blk < HEAD_SLOT - HEAD_DIM:
                pieces.append(jnp.zeros((HEAD_SLOT - HEAD_DIM - n_blk, tq), F32))
            sel_pad = jnp.concatenate(pieces, axis=0)
            sel_q = _dot_nt(eye_ref[...], sel_pad.astype(BF16))
            bias = ((sel_q - 1.0) * -NEG_INF).astype(BF16)
            qa.append(jnp.where(slot_lane < HEAD_DIM, qg[t], jnp.concatenate([bias] * HPG, axis=0)))

        def scores(t, c):
            return _dot_nt(ks_ref[0, gs[t], pl.ds(pl.multiple_of(c * kc, kc), kc), :], qa[t])

        def values(t, c):
            return with_ones(jnp.concatenate([vs_ref[0, gs[t], c * (kc // LANE) + u] for u in range(kc // LANE)], axis=1))

        def chunk(c, carry):
            s = [scores(0, c), scores(1, c)]
            return tuple(flash_update(s[t], values(t, c), carry[t]) for t in range(2))

        init1 = (jnp.full((1, rows), NEG_INF, F32), jnp.zeros((HEAD_DIM + ONES_ROWS, rows), F32))
        carry = lax.fori_loop(0, n_full, chunk, (init1, init1))
        causal = (n_full * kc + lax.broadcasted_iota(jnp.int32, (kc, 1), 0)) <= qpos4
        s_d = [scores(0, n_full), scores(1, n_full)]
        last = [flash_update(jnp.where(causal, s_d[t], NEG_INF), values(t, n_full), carry[t]) for t in range(2)]
        o_s = [normalise(acc) for _, acc in last]

        s_w = [_dot_nt(kw_ref[0, g, pl.ds(w0, w_len), :], qg[t]) for t, g in enumerate(gs)]
        o_w = []
        for t, g in enumerate(gs):
            sm = jnp.where(wmask, s_w[t], NEG_INF)
            e = jnp.exp2(sm - jnp.max(sm, axis=0, keepdims=True))
            v_w = with_ones(jnp.concatenate([vw_ref[0, g, w0 // LANE + u] for u in range(w_len // LANE)], axis=1))
            o_w.append(normalise(_dot(v_w, e.astype(BF16))))

        for t, g in enumerate(gs):
            comb = gate_row(g, 0) * o_c[t] + gate_row(g, 1) * o_s[t] + gate_row(g, 2) * o_w[t]
            heads += [comb[:, j * tq:(j + 1) * tq] for j in range(HPG)]

    o_ref[0] = jnp.concatenate(heads, axis=0).T.astype(BF16)


def _attn_prompt(q_pad, gate_t, kc_nat, vc_t, ksel, vsel, kwin, vwin, imp_t, tq=256, kc=512):
    B, T, _ = q_pad.shape
    n_cmp = kc_nat.shape[2]
    eye = jnp.eye(tq, dtype=BF16)
    b4 = lambda b, i: (b, 0, 0, 0)
    b5 = lambda b, i: (b, 0, 0, 0, 0)
    return pl.pallas_call(
        functools.partial(_attn_prompt_kernel, tq=tq, kc=kc, t_len=T),
        grid=(B, T // tq),
        in_specs=[
            pl.BlockSpec((1, tq, N_HEADS * HEAD_SLOT), lambda b, i: (b, i, 0)),
            pl.BlockSpec((1, GATE_LANES, tq), lambda b, i: (b, 0, i)),
            pl.BlockSpec((1, N_KV, n_cmp, HEAD_SLOT), b4),
            pl.BlockSpec((1, N_KV, HEAD_DIM, n_cmp), b4),
            pl.BlockSpec((1, N_KV, T, HEAD_SLOT), b4),
            pl.BlockSpec((1, N_KV, T // LANE, HEAD_DIM, LANE), b5),
            pl.BlockSpec((1, N_KV, T, HEAD_SLOT), b4),
            pl.BlockSpec((1, N_KV, T // LANE, HEAD_DIM, LANE), b5),
            pl.BlockSpec(imp_t.shape, lambda b, i: (0, 0)),
            pl.BlockSpec(eye.shape, lambda b, i: (0, 0)),
        ],
        out_specs=pl.BlockSpec((1, tq, D_MODEL), lambda b, i: (b, i, 0)),
        out_shape=jax.ShapeDtypeStruct((B, T, D_MODEL), BF16),
        compiler_params=_cparams(("arbitrary", "arbitrary")),
        name="attn_prompt",
    )(q_pad, gate_t, kc_nat, vc_t, ksel, vsel, kwin, vwin, imp_t, eye)


def _sample_cmp_kernel(q_ref, kt_ref, vt_ref, imp_ref, oc_ref, idx_ref, val_ref, *, n_valid, cur, n_blk_lanes):
    nb, n_cmp = kt_ref.shape[0], kt_ref.shape[3]
    n_rows = nb * N_KV
    cmask = lax.broadcasted_iota(jnp.int32, (1, n_cmp), 1) < n_valid
    sums = []
    for bi in range(nb):
        for g in range(N_KV):
            p = _masked_softmax(_dot(q_ref[bi, g].astype(BF16), kt_ref[bi, g]), cmask, -1)
            oc_ref[bi, g] = _dot_nt(p.astype(BF16), vt_ref[bi, g])
            sums.append(jnp.sum(p[0:HPG], axis=0, keepdims=True))
    imp = _dot_exact_lhs(jnp.concatenate(sums, axis=0), imp_ref[...])
    blk = lax.broadcasted_iota(jnp.int32, (1, n_blk_lanes), 1)
    blk_f = blk.astype(F32)
    valid = blk <= cur
    forced = (blk == 0) | (blk == cur) | (blk == cur - 1)
    score = jnp.where(forced, FORCE_SCORE, jnp.where(valid, imp, -1.0))
    score = jnp.where(valid, score, -2.0)
    out_lane = lax.broadcasted_iota(jnp.int32, (1, LANE), 1)
    idx_acc = jnp.zeros((n_rows, LANE), F32)
    val_acc = jnp.full((n_rows, LANE), -1.0, F32)
    for t in range(N_SEL):
        mx = jnp.max(score, axis=-1, keepdims=True)
        ix = jnp.min(jnp.where(score == mx, blk_f, float(n_blk_lanes)), axis=-1, keepdims=True)
        idx_acc = jnp.where(out_lane == t, ix, idx_acc)
        val_acc = jnp.where(out_lane == t, mx, val_acc)
        score = jnp.where(blk_f == ix, -3.0, score)
    idx_ref[...] = idx_acc.astype(jnp.int32)
    val_ref[...] = val_acc


def _sample_cmp(q4, kc_t, vc_t, imp_mat, n_valid, cur, nb=8):
    DB = q4.shape[0]
    n_cmp = kc_t.shape[3]
    nb = nb if DB % nb == 0 else DB
    b4 = lambda b: (b, 0, 0, 0)
    return pl.pallas_call(
        functools.partial(_sample_cmp_kernel, n_valid=n_valid, cur=cur, n_blk_lanes=imp_mat.shape[1]),
        grid=(DB // nb,),
        in_specs=[
            pl.BlockSpec((nb, N_KV, JPAD, HEAD_DIM), b4),
            pl.BlockSpec((nb, N_KV, HEAD_DIM, n_cmp), b4),
            pl.BlockSpec((nb, N_KV, HEAD_DIM, n_cmp), b4),
            pl.BlockSpec(imp_mat.shape, lambda b: (0, 0)),
        ],
        out_specs=[
            pl.BlockSpec((nb, N_KV, JPAD, HEAD_DIM), b4),
            pl.BlockSpec((nb * N_KV, LANE), lambda b: (b, 0)),
            pl.BlockSpec((nb * N_KV, LANE), lambda b: (b, 0)),
        ],
        out_shape=[jax.ShapeDtypeStruct((DB, N_KV, JPAD, HEAD_DIM), F32),
                   jax.ShapeDtypeStruct((DB * N_KV, LANE), jnp.int32),
                   jax.ShapeDtypeStruct((DB * N_KV, LANE), F32)],
        compiler_params=_cparams(("arbitrary",)),
        name="sample_cmp_topk",
    )(q4, kc_t, vc_t, imp_mat)


def _sample_sel_kernel(pg_ref, idx_ref, ok_ref, *refs, qpos, nb_past, kstep):
    del pg_ref
    page_refs = refs[:N_KV * kstep]
    q_ref, gate_ref, selnew_ref, win_ref, winnew_ref, oc_ref, o_ref, m_sc, l_sc, acc_sc = refs[N_KV * kstep:]
    b = pl.program_id(0)
    k = pl.program_id(1)
    bpp = PAGE_SIZE // L_SEL

    @pl.when(k == 0)
    def _():
        m_sc[...] = jnp.full(m_sc.shape, NEG_INF, F32)
        l_sc[...] = jnp.zeros_like(l_sc)
        acc_sc[...] = jnp.zeros_like(acc_sc)

    lane = lax.broadcasted_iota(jnp.int32, (1, PAGE_SIZE), 1)
    for g in range(N_KV):
        k_parts, v_parts, keep_parts = [], [], []
        for kk in range(kstep):
            bref = page_refs[g * kstep + kk]
            n = (b * N_KV + g) * N_SEL + k * kstep + kk
            idx = idx_ref[n]
            tail = idx >= nb_past
            new_col = (lane + jnp.where(tail, 0, PAGE_SIZE)) == 0
            k_parts.append(jnp.where(new_col, selnew_ref[0, 0, g], bref[0, 0, 0]).astype(BF16))
            v_parts.append(jnp.where(new_col, selnew_ref[0, 1, g], bref[0, 1, 0]).astype(BF16))
            first_pos = jnp.where(tail, idx * L_SEL, (idx // bpp) * PAGE_SIZE)
            half = jnp.where(tail, 0, idx % bpp)
            kpos = first_pos + lane + jnp.where(ok_ref[n] > 0, 0, qpos + 1)
            keep_parts.append(jnp.where((lane // L_SEL == half) & (kpos <= qpos), 1.0, 0.0))
        keep = jnp.concatenate(keep_parts, axis=1) > 0.5
        v_t = jnp.concatenate(v_parts, axis=1)
        s = _dot(q_ref[0, g].astype(BF16), jnp.concatenate(k_parts, axis=1))
        sm = jnp.where(keep, s, NEG_INF)
        m_new = jnp.maximum(m_sc[g], jnp.max(sm, axis=-1, keepdims=True))
        alpha = jnp.exp(m_sc[g] - m_new)
        e = jnp.where(keep, jnp.exp(sm - m_new), 0.0)
        l_sc[g] = alpha * l_sc[g] + jnp.sum(e, axis=-1, keepdims=True)
        acc_sc[g] = alpha * acc_sc[g] + _dot_nt(e.astype(BF16), v_t)
        m_sc[g] = m_new

    @pl.when(k == pl.num_programs(1) - 1)
    def _():
        n_win = win_ref.shape[4]
        ext_lane = lax.broadcasted_iota(jnp.int32, (1, LANE), 1)
        wpos = qpos - n_win + lax.broadcasted_iota(jnp.int32, (1, n_win + LANE), 1)
        wmask = (wpos >= 0) & (wpos <= qpos) & (wpos >= qpos - WINDOW)
        for g in range(N_KV):
            k_t = jnp.concatenate([win_ref[0, 0, g], jnp.where(ext_lane == 0, winnew_ref[0, 0, g], 0.0)], axis=1)
            v_t = jnp.concatenate([win_ref[0, 1, g], jnp.where(ext_lane == 0, winnew_ref[0, 1, g], 0.0)], axis=1)
            p_w = _masked_softmax(_dot(q_ref[0, g].astype(BF16), k_t.astype(BF16)), wmask, -1)
            o_w = _dot_nt(p_w.astype(BF16), v_t.astype(BF16))
            gate = gate_ref[0, g]
            o_ref[0, g] = gate[:, 0:1] * oc_ref[0, g] + gate[:, 1:2] * (acc_sc[g] / l_sc[g]) + gate[:, 2:3] * o_w


def _sample_sel(pages, idx, ok, cache_t, q4, gate4, sel_new_t, win_t, win_new_t, o_c, qpos, nb_past, kstep=N_SEL):
    DB = q4.shape[0]
    n_win = win_t.shape[4]

    def page_spec(g, kk):
        return pl.BlockSpec((1, 2, 1, HEAD_DIM, PAGE_SIZE),
                            lambda b, k, pg_ref, idx_ref, ok_ref: (pg_ref[(b * N_KV + g) * N_SEL + k * kstep + kk], 0, g, 0, 0))

    b4 = lambda b, k, *_: (b, 0, 0, 0)
    b5 = lambda b, k, *_: (b, 0, 0, 0, 0)
    grid_spec = pltpu.PrefetchScalarGridSpec(
        num_scalar_prefetch=3,
        grid=(DB, N_SEL // kstep),
        in_specs=[page_spec(g, kk) for g in range(N_KV) for kk in range(kstep)] + [
            pl.BlockSpec((1, N_KV, JPAD, HEAD_DIM), b4),
            pl.BlockSpec((1, N_KV, JPAD, N_BRANCH), b4),
            pl.BlockSpec((1, 2, N_KV, HEAD_DIM, 1), b5),
            pl.BlockSpec((1, 2, N_KV, HEAD_DIM, n_win), b5),
            pl.BlockSpec((1, 2, N_KV, HEAD_DIM, 1), b5),
            pl.BlockSpec((1, N_KV, JPAD, HEAD_DIM), b4),
        ],
        out_specs=pl.BlockSpec((1, N_KV, JPAD, HEAD_DIM), b4),
        scratch_shapes=[pltpu.VMEM((N_KV, JPAD, 1), F32), pltpu.VMEM((N_KV, JPAD, 1), F32),
                        pltpu.VMEM((N_KV, JPAD, HEAD_DIM), F32)],
    )
    return pl.pallas_call(
        functools.partial(_sample_sel_kernel, qpos=qpos, nb_past=nb_past, kstep=kstep),
        grid_spec=grid_spec,
        out_shape=jax.ShapeDtypeStruct((DB, N_KV, JPAD, HEAD_DIM), F32),
        compiler_params=_cparams(("arbitrary", "arbitrary")),
        name="sample_sel_win",
    )(pages, idx, ok, *([cache_t] * (N_KV * kstep)), q4, gate4, sel_new_t, win_t, win_new_t, o_c)


def _imp_matrix(n_cmp_rows, n_blk_cols):
    m = np.zeros((n_cmp_rows, n_blk_cols), np.float32)
    for j in range(n_blk_cols):
        for a in range(SEL_RATIO):
            for c in range(CMP_PARTS):
                i = SEL_RATIO * j + a - c
                if 0 <= i < n_cmp_rows:
                    m[i, j] += 1.0
    return m


def _rope_angles(pos):
    inv = ROPE_THETA ** (-jnp.arange(0, ROT_DIM, 2, dtype=F32) / ROT_DIM)
    ang = pos.astype(F32)[:, None] * inv[None, :]
    return jnp.cos(ang), jnp.sin(ang)


def _rope_tables(pos, period):
    cos, sin = _rope_angles(pos)
    n = pos.shape[0]
    rest1, rest0, zh = jnp.ones((n, period - ROT_DIM), F32), jnp.zeros((n, period - ROT_DIM), F32), jnp.zeros_like(sin)
    c = jnp.concatenate([cos, cos, rest1], axis=1)
    sa = jnp.concatenate([-sin, zh, rest0], axis=1)
    sb = jnp.concatenate([zh, sin, rest0], axis=1)
    return tuple(jnp.tile(t, (1, LANE // period)) for t in (c, sa, sb))


def _to_rows(kv_t):
    return jnp.transpose(kv_t, (0, 4, 1, 2, 3))


def _to_cols(kv):
    return jnp.transpose(kv, (0, 2, 3, 4, 1))


def kernel(x_prompt, x_sample, state_pool, cache_cmp_kv, cache_sel_kv, state_win_kv, page_table,
           norm_mix, norm_ffn, pool_w, pool_scale, w_qg, w_o, norm_kv, w_kv, cmp_pe, cmp_w1, cmp_w2,
           mlp_up, mlp_down, norm_final):
    B, T, D = x_prompt.shape
    DB, S, _ = x_sample.shape
    past_len = page_table.shape[1] * PAGE_SIZE
    n_blk_p = T // L_SEL
    assert S == 1 and D == D_MODEL and T % PAGE_SIZE == 0 and n_blk_p <= HEAD_SLOT - HEAD_DIM and n_blk_p % SUBLANE == 0

    n_q = N_HEADS * HEAD_DIM
    w_q = w_qg[0][:, :n_q]
    w_gate = jnp.pad(w_qg[0][:, n_q:], ((0, 0), (0, GATE_LANES - N_HEADS * N_BRANCH)))
    w_kv5 = w_kv.reshape(D, N_BRANCH, 2, N_KV * HEAD_DIM)
    w_k_sel_win = jnp.concatenate([w_kv5[:, 1, 0], w_kv5[:, 2, 0]], axis=1).astype(BF16)
    w_kv_b = w_kv.astype(BF16)
    w_o_b = w_o[0].astype(BF16)
    pool_w_b = pool_w[0].astype(BF16)
    up_b, down_b = mlp_up.astype(BF16), mlp_down.astype(BF16)
    g_mix0, g_mix1 = norm_mix[0:1], norm_mix[1:2]
    g_kv, g_fin = norm_kv[None, :], norm_final[None, :]
    w1_6 = cmp_w1.reshape(2, CMP_PARTS, STRIDE // 2, 2, HEAD_DIM, CMP_HIDDEN)
    w1pair = jnp.einsum('ab,eprshk->ersahbpk', jnp.eye(2, dtype=F32), w1_6)
    w1pair = w1pair.reshape(2, STRIDE // 2, 2 * 2 * HEAD_DIM, 2 * CMP_PARTS * CMP_HIDDEN).astype(BF16)
    w1_flat = cmp_w1.reshape(2, L_CMP * HEAD_DIM, CMP_HIDDEN).astype(BF16)
    pe8 = jnp.broadcast_to(cmp_pe.transpose(1, 0, 2).reshape(2, 1, L_CMP * HEAD_DIM), (2, SUBLANE, L_CMP * HEAD_DIM)).astype(BF16)
    w2pad = jnp.pad(cmp_w2[0], ((0, 0), (0, HEAD_SLOT - HEAD_DIM))).astype(BF16)
    w2t = cmp_w2.transpose(0, 2, 1).astype(BF16)

    h1, pool_tail = _pool_prompt(x_prompt, g_mix0, pool_w_b, pool_scale)
    pool_prompt = pool_tail[None, :, POOL_HALO - POOL_BUF:]
    h2 = _mlp(h1.reshape(B * T, D), norm_ffn[0:1], up_b[0], down_b[0], g_fin)
    pos_p = jnp.arange(T)
    cos_p, sin_p = _rope_angles(pos_p)
    cmp_t, sel_t, win_t, ksel, kwin, vsel, vwin, q_pad, gate_t = _proj_prompt(
        h2.reshape(B, T, D), g_kv, g_mix1, w_kv_b.T, w_k_sel_win, w_q.astype(BF16), w_gate.T.astype(BF16),
        _rope_tables(pos_p, HEAD_DIM), (cos_p.T, sin_p.T))
    kc_nat_p, _, vc_t_p = _cmp_finish(_cmp_parts(cmp_t, w1pair), pe8, w1_flat, w2pad, w2t)
    imp_t = jnp.asarray(_imp_matrix(kc_nat_p.shape[2], n_blk_p).T, BF16)
    o_p = _attn_prompt(q_pad, gate_t, kc_nat_p, vc_t_p, ksel, vsel, kwin, vwin, imp_t)
    y_p = _mlp(h2, norm_ffn[1:2], up_b[1], down_b[1], g_fin, o=o_p.reshape(B * T, D), w_o=w_o_b, final_norm=True)

    hist = jnp.concatenate([jnp.zeros((1, DB, D), F32), state_pool[0].transpose(1, 0, 2)], axis=0)
    hs1, us = _pool_sample(x_sample[:, 0], hist, g_mix0, pool_w_b, pool_scale)
    pool_sample = jnp.concatenate([state_pool[0][:, 1:], us[:, None]], axis=1)[None]
    hs2 = _mlp(hs1, norm_ffn[0:1], up_b[0], down_b[0], g_fin)
    cmp_s, sel_s, win_s, q_s, gate_s = _proj_sample(
        hs2, g_kv, g_mix1, w_kv_b, w_q.astype(BF16), w_gate.astype(BF16),
        _rope_tables(jnp.full((DB,), past_len, jnp.int32), HEAD_DIM))
    parts_s = _cmp_parts(_to_cols(cache_cmp_kv), w1pair, pages=page_table.reshape(-1), n_b=DB)
    _, kc_t_s, vc_t_s = _cmp_finish(parts_s, pe8, w1_flat, w2pad, w2t)
    n_valid = past_len // STRIDE - CMP_PARTS + 1
    nb_past = past_len // L_SEL
    n_blk_lanes = -(-(nb_past + 1) // LANE) * LANE
    jpad = ((0, 0), (0, 0), (0, JPAD - HPG), (0, 0))
    q4 = jnp.pad(q_s.astype(F32).reshape(DB, N_KV, HPG, HEAD_DIM), jpad)
    gate4 = jnp.pad(gate_s[:, :N_HEADS * N_BRANCH].reshape(DB, N_KV, HPG, N_BRANCH), jpad)
    imp_s = jnp.asarray(_imp_matrix(kc_t_s.shape[3], n_blk_lanes), BF16)
    o_c, idx_pad, val_pad = _sample_cmp(q4, kc_t_s, vc_t_s, imp_s, n_valid, nb_past)
    idx = idx_pad[:, :N_SEL].reshape(DB, N_KV, N_SEL)
    ok = (val_pad[:, :N_SEL] >= 0).astype(jnp.int32).reshape(DB, N_KV, N_SEL)
    bpp = PAGE_SIZE // L_SEL
    logical_page = (jnp.minimum(idx, nb_past - 1) // bpp).reshape(DB, -1)
    pages = jnp.take_along_axis(page_table, logical_page, axis=1)
    kv5 = lambda a, n, t: a.reshape(n, t, 2, N_KV, HEAD_DIM)
    o_s = _sample_sel(pages.reshape(-1), idx.reshape(-1), ok.reshape(-1), _to_cols(cache_sel_kv), q4, gate4,
                      _to_cols(kv5(sel_s, DB, 1)), _to_cols(state_win_kv), _to_cols(kv5(win_s, DB, 1)), o_c,
                      past_len, nb_past)
    o_s = o_s[:, :, :HPG].reshape(DB, D).astype(BF16)
    y_s = _mlp(hs2, norm_ffn[1:2], up_b[1], down_b[1], g_fin, o=o_s, w_o=w_o_b, final_norm=True)

    win_sample = jnp.concatenate([state_win_kv, kv5(win_s, DB, 1)], axis=1)[:, S:]
    return (y_p.reshape(B, T, D), y_s.reshape(DB, S, D),
            _to_rows(cmp_t), _to_rows(sel_t), _to_rows(win_t[..., T - min(WINDOW, T):]), pool_prompt,
            kv5(cmp_s, DB, 1), kv5(sel_s, DB, 1), win_sample, pool_sample)
```
